```python
import math
import jax
import jax.numpy as jnp
from jax import lax
import numpy as np

D_MODEL = 1024
BATCH = 4
SEQ = 4096
DEPTH = 4

GRID_W = 64
CTX_LEN = 256
N_MOD = 9
D_FF = 256 * (-(-8 * D_MODEL // (3 * 256)))
CONV_K = 5
EPS = 1e-6
N_BRANCH = 4
BRANCH_W = D_MODEL // 2
SSM_P = 64
SSM_H = BRANCH_W // SSM_P
SSM_G = 2
SSM_N = 128
SSM_CHUNK = 128
SSM_W = SSM_H * SSM_P
SSM_CONV_CH = SSM_W + 2 * SSM_G * SSM_N
ML_DV = 128
ML_H = BRANCH_W // ML_DV
ML_DK = ML_DV // 2
ML_CHUNK = 64
S5_GC = 16
S5_GROUPS = BRANCH_W // S5_GC
S5_P = 64
GDN_DK = 128
GDN_DV = 128
GDN_H = BRANCH_W // GDN_DV
GDN_CHUNK = 64
IN_SPLITS = (SSM_W, SSM_CONV_CH, 2 * SSM_H,
             2 * ML_H * ML_DK, ML_H * ML_DV, ML_H * ML_DV, 4 * ML_H,
             S5_GROUPS * S5_GC,
             GDN_H * (2 * GDN_DK + GDN_DV), GDN_H * GDN_DV, 2 * GDN_H, 2 * GDN_H,
             N_BRANCH * D_MODEL)
D_IN = sum(IN_SPLITS)

kernel_name = 'hybrid_flow_backbone'


def rmsnorm(x, w):
    xf = x.astype(jnp.float32)
    y = xf * lax.rsqrt(jnp.mean(xf * xf, axis=-1, keepdims=True) + EPS)
    return (y * w.astype(jnp.float32)).astype(x.dtype)


def l2norm(a):
    return a * lax.rsqrt(jnp.sum(a * a, axis=-1, keepdims=True) + EPS)


def modulate(x, shift, scale):
    return x * (1 + scale) + shift


def swiglu(x, w_up, w_down):
    g, u = jnp.split(x @ w_up, 2, axis=-1)
    return (jax.nn.silu(g) * u) @ w_down


def split_cols(a, sizes):
    return jnp.split(a, np.cumsum(sizes)[:-1].tolist(), axis=-1)


def dwconv(x, w, b=None):
    y = lax.conv_general_dilated(x, w[:, None, :].astype(x.dtype), window_strides=(1,),
                                 padding=[(CONV_K // 2, CONV_K // 2)],
                                 dimension_numbers=('NWC', 'WIO', 'NWC'),
                                 feature_group_count=x.shape[-1])
    return y if b is None else y + b.astype(x.dtype)


def raster_to_col(a, rows):
    bsz, n, d = a.shape
    return a.reshape(bsz, rows, GRID_W, d).transpose(0, 2, 1, 3).reshape(bsz, n, d)


def col_to_raster(a, rows):
    bsz, n, d = a.shape
    return a.reshape(bsz, GRID_W, rows, d).transpose(0, 2, 1, 3).reshape(bsz, n, d)


def to_chunks(a, q):
    return a.reshape(a.shape[0], a.shape[1] // q, q, *a.shape[2:])


def from_chunks(a):
    return a.reshape(a.shape[0], a.shape[1] * a.shape[2], *a.shape[3:])


def masked_diff(gt, strict):
    q = gt.shape[-1]
    idx = jnp.arange(q)
    mask = (idx[:, None] > idx[None, :]) if strict else (idx[:, None] >= idx[None, :])
    return jnp.where(mask, gt[..., :, None] - gt[..., None, :], -jnp.inf)


def two_way(scan_fn, ctx_f, lat_f, par_f, ctx_b, lat_b, par_b, init):
    flip = lambda args: tuple(jnp.flip(a, axis=1) for a in args)
    yc_f, s_f = scan_fn(*ctx_f, *par_f, init)
    yl_f, _ = scan_fn(*lat_f, *par_f, s_f)
    yc_b, s_b = scan_fn(*flip(ctx_b), *par_b, init)
    yl_b, _ = scan_fn(*flip(lat_b), *par_b, s_b)
    return yc_f + jnp.flip(yc_b, axis=1), yl_f + jnp.flip(yl_b, axis=1)


def ssd_scan(xs, dt, bm, cm, a, h0):
    xq, dq, bq, cq = (to_chunks(t, SSM_CHUNK) for t in (xs, dt, bm, cm))
    g = jnp.cumsum(dq * a, axis=2)
    decay = jnp.exp(masked_diff(jnp.swapaxes(g, 2, 3), False))
    xdt = xq * dq[..., None]
    scores = jnp.einsum('bcihn,bcjhn->bchij', cq, bq) * decay
    y_intra = jnp.einsum('bchij,bcjhp->bcihp', scores, xdt)
    g_last = g[:, :, -1]
    states = jnp.einsum('bcjhn,bcjh,bcjhp->bchnp', bq, jnp.exp(g_last[:, :, None] - g), xdt)
    def step(h, inp):
        s, dec = inp
        return dec[:, :, None, None] * h + s, h
    h_fin, h_prev = lax.scan(step, h0, (jnp.moveaxis(states, 1, 0), jnp.moveaxis(jnp.exp(g_last), 1, 0)))
    h_prev = jnp.moveaxis(h_prev, 0, 1)
    y_inter = jnp.einsum('bcihn,bcih,bchnp->bcihp', cq, jnp.exp(g), h_prev)
    return from_chunks(y_intra + y_inter), h_fin


def mlstm_scan(q, k, v, ig, lf, state):
    qq, kq, vq = (to_chunks(t, ML_CHUNK) for t in (q, k, v))
    bt = jnp.swapaxes(jnp.cumsum(to_chunks(lf, ML_CHUNK), axis=2), 2, 3)
    it = jnp.swapaxes(to_chunks(ig, ML_CHUNK), 2, 3)
    dmat = masked_diff(bt, False) + it[..., None, :]
    m_intra = jnp.max(dmat, axis=-1)
    a_end = bt[..., -1:] - bt + it
    m_loc = jnp.max(a_end, axis=-1)
    w_end = jnp.exp(a_end - m_loc[..., None])
    c_loc = jnp.einsum('bchj,bcjhk,bcjhv->bchkv', w_end, kq, vq)
    n_loc = jnp.einsum('bchj,bcjhk->bchk', w_end, kq)
    b_last = bt[..., -1]
    def step(carry, inp):
        c_st, n_st, m_st = carry
        cl, nl, ml, bl = inp
        m_new = jnp.maximum(bl + m_st, ml)
        s_old = jnp.exp(bl + m_st - m_new)
        s_loc = jnp.exp(ml - m_new)
        c_new = s_old[..., None, None] * c_st + s_loc[..., None, None] * cl
        n_new = s_old[..., None] * n_st + s_loc[..., None] * nl
        return (c_new, n_new, m_new), (c_st, n_st, m_st)
    final, prev = lax.scan(step, state, tuple(jnp.moveaxis(t, 1, 0) for t in (c_loc, n_loc, m_loc, b_last)))
    c_prev, n_prev, m_prev = (jnp.moveaxis(t, 0, 1) for t in prev)
    e = bt + m_prev[..., None]
    m_out = jnp.maximum(e, m_intra)
    w_intra = jnp.exp(dmat - m_out[..., None]) * jnp.einsum('bcihk,bcjhk->bchij', qq, kq)
    w_x = jnp.exp(e - m_out)
    num = (jnp.einsum('bchij,bcjhv->bcihv', w_intra, vq)
           + jnp.einsum('bchi,bcihk,bchkv->bcihv', w_x, qq, c_prev))
    den = jnp.sum(w_intra, axis=-1) + w_x * jnp.einsum('bcihk,bchk->bchi', qq, n_prev)
    den = jnp.maximum(jnp.abs(den), jnp.exp(-m_out))
    h = num / jnp.swapaxes(den, 2, 3)[..., None]
    return from_chunks(h), final


def complex_affine(e1, e2):
    a1r, a1i, b1r, b1i = e1
    a2r, a2i, b2r, b2i = e2
    return (a2r * a1r - a2i * a1i, a2r * a1i + a2i * a1r,
            a2r * b1r - a2i * b1i + b2r, a2r * b1i + a2i * b1r + b2i)


def s5_scan(u, lam_re, lam_im, log_dt, b_re, b_im, c_re, c_im, h0):
    f32 = jnp.float32
    lre = jnp.minimum(lam_re.astype(f32), -1e-4)
    lim = lam_im.astype(f32)
    dt = jnp.exp(log_dt.astype(f32))[:, None]
    mag = jnp.exp(lre * dt)
    ab_re, ab_im = mag * jnp.cos(lim * dt), mag * jnp.sin(lim * dt)
    den = lre * lre + lim * lim
    f_re = ((ab_re - 1.0) * lre + ab_im * lim) / den
    f_im = (ab_im * lre - (ab_re - 1.0) * lim) / den
    br, bi = b_re.astype(f32), b_im.astype(f32)
    bb_re = f_re[..., None] * br - f_im[..., None] * bi
    bb_im = f_re[..., None] * bi + f_im[..., None] * br
    bu_re = jnp.einsum('gpc,bsgc->bsgp', bb_re, u)
    bu_im = jnp.einsum('gpc,bsgc->bsgp', bb_im, u)
    h0_re, h0_im = h0
    bu_re = bu_re.at[:, 0].add(ab_re * h0_re - ab_im * h0_im)
    bu_im = bu_im.at[:, 0].add(ab_re * h0_im + ab_im * h0_re)
    n = u.shape[1]
    a_re = jnp.broadcast_to(ab_re, (1, n) + ab_re.shape)
    a_im = jnp.broadcast_to(ab_im, (1, n) + ab_im.shape)
    _, _, h_re, h_im = lax.associative_scan(complex_affine, (a_re, a_im, bu_re, bu_im), axis=1)
    y = (jnp.einsum('gcp,bsgp->bsgc', c_re.astype(f32), h_re)
         - jnp.einsum('gcp,bsgp->bsgc', c_im.astype(f32), h_im))
    return y, (h_re[:, -1], h_im[:, -1])


def gdn_scan(q, k, v, lg, beta, s0):
    qq, kq, vq, bq = (to_chunks(t, GDN_CHUNK) for t in (q, k, v, beta))
    g = jnp.cumsum(to_chunks(lg, GDN_CHUNK), axis=2)
    gt = jnp.swapaxes(g, 2, 3)
    kb = kq * bq[..., None]
    lmat = jnp.einsum('bcihk,bcjhk->bchij', kb, kq) * jnp.exp(masked_diff(gt, True))
    eye = jnp.eye(GDN_CHUNK, dtype=jnp.float32)
    tmat = lax.linalg.triangular_solve(eye + lmat, jnp.broadcast_to(eye, lmat.shape),
                                       left_side=True, lower=True, unit_diagonal=True)
    u_base = jnp.einsum('bchij,bcjhv->bcihv', tmat, vq * bq[..., None])
    w_k = jnp.einsum('bchij,bcjhk->bcihk', tmat, kb * jnp.exp(g)[..., None])
    a_qk = jnp.einsum('bcihk,bcjhk->bchij', qq, kq) * jnp.exp(masked_diff(gt, False))
    q_dec = qq * jnp.exp(g)[..., None]
    g_last = g[:, :, -1]
    k_dec = kq * jnp.exp(g_last[:, :, None] - g)[..., None]
    def step(s, inp):
        ub, wc, qd, aqk, kd, gl = inp
        u = ub - jnp.einsum('bihk,bhkv->bihv', wc, s)
        o = jnp.einsum('bihk,bhkv->bihv', qd, s) + jnp.einsum('bhij,bjhv->bihv', aqk, u)
        s_new = jnp.exp(gl)[..., None, None] * s + jnp.einsum('bjhk,bjhv->bhkv', kd, u)
        return s_new, o
    s_fin, o = lax.scan(step, s0, tuple(jnp.moveaxis(t, 1, 0) for t in (u_base, w_k, q_dec, a_qk, k_dec, g_last)))
    return from_chunks(jnp.moveaxis(o, 0, 1)), s_fin


def mamba_branch(p_ctx, p_lat, need_ctx, conv_w, conv_b, dt_bias, a_log, d_skip, norm_w):
    f32 = jnp.float32
    def prep(z, xbc, dt):
        bsz, n = xbc.shape[:2]
        xbc = jax.nn.silu(dwconv(xbc, conv_w, conv_b)).astype(f32)
        xs, bm, cm = split_cols(xbc, (SSM_W, SSM_G * SSM_N, SSM_G * SSM_N))
        xs = xs.reshape(bsz, n, SSM_H, SSM_P)
        bm = jnp.repeat(bm.reshape(bsz, n, SSM_G, SSM_N), SSM_H // SSM_G, axis=2)
        cm = jnp.repeat(cm.reshape(bsz, n, SSM_G, SSM_N), SSM_H // SSM_G, axis=2)
        dt = jax.nn.softplus(dt.astype(f32).reshape(bsz, n, 2, SSM_H) + dt_bias.astype(f32))
        return z, xs, bm, cm, dt
    zc, xc, bc, cc, dtc = prep(*p_ctx)
    zl, xl, bl, cl, dtl = prep(*p_lat)
    a = -jnp.exp(a_log.astype(f32))
    h0 = jnp.zeros((xl.shape[0], SSM_H, SSM_N, SSM_P), f32)
    yc, yl = two_way(ssd_scan, (xc, dtc[:, :, 0], bc, cc), (xl, dtl[:, :, 0], bl, cl), (a[0],),
                     (xc, dtc[:, :, 1], bc, cc), (xl, dtl[:, :, 1], bl, cl), (a[1],), h0)
    def post(y, xs, z):
        bsz, n = y.shape[:2]
        y = (y + d_skip.astype(f32)[:, None] * xs).reshape(bsz, n, SSM_W)
        return rmsnorm(y * jax.nn.silu(z.astype(f32)), norm_w)
    return (post(yc, xc, zc) if need_ctx else None), post(yl, xl, zl)


def mlstm_branch(p_ctx, p_lat, need_ctx, conv_w, conv_b, gate_b, norm_w):
    f32 = jnp.float32
    def prep(qk, v, o, gates):
        bsz, n = v.shape[:2]
        qk = jax.nn.silu(dwconv(qk, conv_w, conv_b)).astype(f32)
        q, k = jnp.split(qk, 2, axis=-1)
        q = q.reshape(bsz, n, ML_H, ML_DK) * ML_DK ** -0.5
        k = k.reshape(bsz, n, ML_H, ML_DK)
        v = v.astype(f32).reshape(bsz, n, ML_H, ML_DV)
        gg = gates.astype(f32).reshape(bsz, n, 2, 2, ML_H) + gate_b.astype(f32)
        return o, q, k, v, gg[:, :, :, 0], jax.nn.log_sigmoid(gg[:, :, :, 1])
    oc, qc, kc, vc, igc, lfc = prep(*p_ctx)
    ol, ql, kl, vl, igl, lfl = prep(*p_lat)
    bsz = ql.shape[0]
    state0 = (jnp.zeros((bsz, ML_H, ML_DK, ML_DV), f32), jnp.zeros((bsz, ML_H, ML_DK), f32),
              jnp.zeros((bsz, ML_H), f32))
    yc, yl = two_way(mlstm_scan, (qc, kc, vc, igc[:, :, 0], lfc[:, :, 0]), (ql, kl, vl, igl[:, :, 0], lfl[:, :, 0]), (),
                     (qc, kc, vc, igc[:, :, 1], lfc[:, :, 1]), (ql, kl, vl, igl[:, :, 1], lfl[:, :, 1]), (), state0)
    def post(h, o):
        bsz_, n = h.shape[:2]
        h = rmsnorm(h, norm_w.reshape(ML_H, ML_DV)).reshape(bsz_, n, ML_H * ML_DV)
        return jax.nn.sigmoid(o.astype(f32)) * h
    return (post(yc, oc) if need_ctx else None), post(yl, ol)


def s5_branch(u_ctx, u_lat, need_ctx, a_re, a_im, log_dt, b_re, b_im, c_re, c_im, d_skip, glu_w, glu_b):
    f32 = jnp.float32
    def prep(u):
        return u.astype(f32).reshape(u.shape[0], u.shape[1], S5_GROUPS, S5_GC)
    uc, ul = prep(u_ctx), prep(u_lat)
    zero = jnp.zeros((ul.shape[0], S5_GROUPS, S5_P), f32)
    par_f = (a_re[0], a_im[0], log_dt[0], b_re, b_im, c_re, c_im)
    par_b = (a_re[1], a_im[1], log_dt[1], b_re, b_im, c_re, c_im)
    yc, yl = two_way(s5_scan, (uc,), (ul,), par_f, (uc,), (ul,), par_b, (zero, zero))
    def post(y, u):
        bsz, n = y.shape[:2]
        y = (y + d_skip.astype(f32).reshape(S5_GROUPS, S5_GC) * u).reshape(bsz, n, BRANCH_W)
        g = jax.nn.gelu(y)
        return g * jax.nn.sigmoid(g @ glu_w.astype(f32) + glu_b.astype(f32))
    return (post(yc, uc) if need_ctx else None), post(yl, ul)


def gdn_branch(p_ctx, p_lat, need_ctx, conv_w, dt_bias, a_log, norm_w):
    f32 = jnp.float32
    def prep(qkv, z, a, beta):
        bsz, n = qkv.shape[:2]
        qkv = jax.nn.silu(dwconv(qkv, conv_w)).astype(f32)
        q, k, v = split_cols(qkv, (GDN_H * GDN_DK, GDN_H * GDN_DK, GDN_H * GDN_DV))
        q = l2norm(q.reshape(bsz, n, GDN_H, GDN_DK)) * GDN_DK ** -0.5
        k = l2norm(k.reshape(bsz, n, GDN_H, GDN_DK))
        v = v.reshape(bsz, n, GDN_H, GDN_DV)
        lg = -jnp.exp(a_log.astype(f32)) * jax.nn.softplus(a.astype(f32).reshape(bsz, n, 2, GDN_H) + dt_bias.astype(f32))
        bt = jax.nn.sigmoid(beta.astype(f32).reshape(bsz, n, 2, GDN_H))
        return z, q, k, v, lg, bt
    zc, qc, kc, vc, lgc, btc = prep(*p_ctx)
    zl, ql, kl, vl, lgl, btl = prep(*p_lat)
    s0 = jnp.zeros((ql.shape[0], GDN_H, GDN_DK, GDN_DV), f32)
    yc, yl = two_way(gdn_scan, (qc, kc, vc, lgc[:, :, 0], btc[:, :, 0]), (ql, kl, vl, lgl[:, :, 0], btl[:, :, 0]), (),
                     (qc, kc, vc, lgc[:, :, 1], btc[:, :, 1]), (ql, kl, vl, lgl[:, :, 1], btl[:, :, 1]), (), s0)
    def post(o, z):
        bsz, n = o.shape[:2]
        o = rmsnorm(o, norm_w) * jax.nn.silu(z.astype(f32).reshape(bsz, n, GDN_H, GDN_DV))
        return o.reshape(bsz, n, GDN_H * GDN_DV)
    return (post(yc, zc) if need_ctx else None), post(yl, zl)


def merge(ys, gate_logits, gate_b, w_branch, w_out, dtype):
    bsz, n = gate_logits.shape[:2]
    gates = jax.nn.sigmoid(gate_logits.reshape(bsz, n, N_BRANCH, D_MODEL) + gate_b)
    ys = jnp.stack(ys, axis=2).astype(dtype)
    merged = jnp.einsum('bskd,bskw,kwd->bsd', gates, ys, w_branch)
    return merged @ w_out


def token_mixers(xl, xc, need_ctx, w_in, ssm_conv_w, ssm_conv_b, ssm_dt_bias, ssm_a_log, ssm_d, ssm_norm,
                 ml_conv_w, ml_conv_b, ml_gate_b, ml_norm, s5_a_re, s5_a_im, s5_log_dt, s5_b_re, s5_b_im,
                 s5_c_re, s5_c_im, s5_d, s5_glu_w, s5_glu_b, gdn_conv_w, gdn_dt_bias, gdn_a_log, gdn_norm,
                 gate_b, w_branch, w_out):
    pl = split_cols(xl @ w_in, IN_SPLITS)
    pc = split_cols(xc @ w_in, IN_SPLITS)
    ya_c, ya_l = mamba_branch(pc[0:3], pl[0:3], need_ctx, ssm_conv_w, ssm_conv_b, ssm_dt_bias, ssm_a_log, ssm_d, ssm_norm)
    yb_c, yb_l = mlstm_branch(pc[3:7], pl[3:7], need_ctx, ml_conv_w, ml_conv_b, ml_gate_b, ml_norm)
    yc_c, yc_l = s5_branch(pc[7], pl[7], need_ctx, s5_a_re, s5_a_im, s5_log_dt, s5_b_re, s5_b_im,
                           s5_c_re, s5_c_im, s5_d, s5_glu_w, s5_glu_b)
    yd_c, yd_l = gdn_branch(pc[8:12], pl[8:12], need_ctx, gdn_conv_w, gdn_dt_bias, gdn_a_log, gdn_norm)
    y_lat = merge((ya_l, yb_l, yc_l, yd_l), pl[12], gate_b, w_branch, w_out, xl.dtype)
    y_ctx = merge((ya_c, yb_c, yc_c, yd_c), pc[12], gate_b, w_branch, w_out, xc.dtype) if need_ctx else None
    return y_lat, y_ctx


def setup_inputs(seed: int = 0) -> dict:
    key = jax.random.key(seed)
    ks = iter(jax.random.split(key, 64))
    f32 = jnp.float32
    def nrm(shape, scale):
        return scale * jax.random.normal(next(ks), shape, f32)
    def unif(shape, lo, hi):
        return jax.random.uniform(next(ks), shape, f32, lo, hi)
    def dt_bias(shape):
        dt = jnp.exp(unif(shape, math.log(1e-3), math.log(1e-1)))
        return dt + jnp.log(-jnp.expm1(-dt))
    d = D_MODEL
    ig_b = nrm((DEPTH, 2, 1, ML_H), 0.1)
    fg_b = jnp.linspace(3.0, 6.0, ML_H, dtype=f32) + nrm((DEPTH, 2, 1, ML_H), 0.1)
    return {
        'x': nrm((BATCH, SEQ, d), 1.0),
        'c': nrm((BATCH, d), 1.0),
        'ctx': nrm((BATCH, CTX_LEN, d), 1.0),
        'c_ctx': nrm((d,), 1.0),
        'ada_w': nrm((DEPTH, d, N_MOD * d), 0.5 * d ** -0.5),
        'ada_b': nrm((DEPTH, N_MOD * d), 0.02),
        'norm_w': 1.0 + nrm((DEPTH, 3, d), 0.02),
        'ffn_up': nrm((DEPTH, 2, d, 2 * D_FF), d ** -0.5),
        'ffn_down': nrm((DEPTH, 2, D_FF, d), D_FF ** -0.5),
        'w_in': nrm((DEPTH, d, D_IN), d ** -0.5),
        'ssm_conv_w': nrm((DEPTH, CONV_K, SSM_CONV_CH), CONV_K ** -0.5),
        'ssm_conv_b': nrm((DEPTH, SSM_CONV_CH), 0.02),
        'ssm_dt_bias': dt_bias((DEPTH, 2, SSM_H)),
        'ssm_a_log': jnp.log(unif((DEPTH, 2, SSM_H), 1.0, 16.0)),
        'ssm_d': 1.0 + nrm((DEPTH, SSM_H), 0.02),
        'ssm_norm': 1.0 + nrm((DEPTH, SSM_W), 0.02),
        'ml_conv_w': nrm((DEPTH, CONV_K, 2 * ML_H * ML_DK), CONV_K ** -0.5),
        'ml_conv_b': nrm((DEPTH, 2 * ML_H * ML_DK), 0.02),
        'ml_gate_b': jnp.concatenate([ig_b, fg_b], axis=2),
        'ml_norm': 1.0 + nrm((DEPTH, ML_H * ML_DV), 0.02),
        's5_a_re': -0.5 + nrm((DEPTH, 2, S5_GROUPS, S5_P), 0.01),
        's5_a_im': jnp.pi * jnp.arange(S5_P, dtype=f32) + nrm((DEPTH, 2, S5_GROUPS, S5_P), 0.01),
        's5_log_dt': unif((DEPTH, 2, S5_GROUPS), math.log(1e-3), math.log(1e-1)),
        's5_b_re': nrm((DEPTH, S5_GROUPS, S5_P, S5_GC), 0.7 * S5_GC ** -0.5),
        's5_b_im': nrm((DEPTH, S5_GROUPS, S5_P, S5_GC), 0.7 * S5_GC ** -0.5),
        's5_c_re': nrm((DEPTH, S5_GROUPS, S5_GC, S5_P), 0.7 * S5_P ** -0.5),
        's5_c_im': nrm((DEPTH, S5_GROUPS, S5_GC, S5_P), 0.7 * S5_P ** -0.5),
        's5_d': nrm((DEPTH, BRANCH_W), 1.0),
        's5_glu_w': nrm((DEPTH, BRANCH_W, BRANCH_W), BRANCH_W ** -0.5),
        's5_glu_b': nrm((DEPTH, BRANCH_W), 0.02),
        'gdn_conv_w': nrm((DEPTH, CONV_K, GDN_H * (2 * GDN_DK + GDN_DV)), CONV_K ** -0.5),
        'gdn_dt_bias': dt_bias((DEPTH, 2, GDN_H)),
        'gdn_a_log': jnp.log(unif((DEPTH, 2, GDN_H), 1.0, 16.0)),
        'gdn_norm': 1.0 + nrm((DEPTH, GDN_DV), 0.02),
        'gate_b': nrm((DEPTH, N_BRANCH, d), 0.02),
        'w_branch': nrm((DEPTH, N_BRANCH, BRANCH_W, d), BRANCH_W ** -0.5),
        'w_out': nrm((DEPTH, d, d), d ** -0.5),
        'final_norm': 1.0 + nrm((d,), 0.02),
    }


def reference(x, c, ctx, c_ctx, ada_w, ada_b, norm_w, ffn_up, ffn_down, w_in, ssm_conv_w, ssm_conv_b,
              ssm_dt_bias, ssm_a_log, ssm_d, ssm_norm, ml_conv_w, ml_conv_b, ml_gate_b, ml_norm,
              s5_a_re, s5_a_im, s5_log_dt, s5_b_re, s5_b_im, s5_c_re, s5_c_im, s5_d, s5_glu_w, s5_glu_b,
              gdn_conv_w, gdn_dt_bias, gdn_a_log, gdn_norm, gate_b, w_branch, w_out, final_norm):
    rows = x.shape[1] // GRID_W
    h_lat, h_ctx = x, ctx
    s_c, s_cc = jax.nn.silu(c), jax.nn.silu(c_ctx)
    for l in range(DEPTH):
        need_ctx = l < DEPTH - 1
        mod = jnp.split((s_c @ ada_w[l] + ada_b[l])[:, None, :], N_MOD, axis=-1)
        mod_c = jnp.split(s_cc @ ada_w[l] + ada_b[l], N_MOD, axis=-1)
        h_lat = h_lat + 0.5 * mod[2] * swiglu(modulate(rmsnorm(h_lat, norm_w[l, 0]), mod[0], mod[1]), ffn_up[l, 0], ffn_down[l, 0])
        h_ctx = h_ctx + 0.5 * mod_c[2] * swiglu(modulate(rmsnorm(h_ctx, norm_w[l, 0]), mod_c[0], mod_c[1]), ffn_up[l, 0], ffn_down[l, 0])
        xl = modulate(rmsnorm(h_lat, norm_w[l, 1]), mod[3], mod[4])
        xc = modulate(rmsnorm(h_ctx, norm_w[l, 1]), mod_c[3], mod_c[4])
        col_order = l % 2 == 1
        if col_order:
            xl = raster_to_col(xl, rows)
        y_lat, y_ctx = token_mixers(xl, xc, need_ctx, w_in[l], ssm_conv_w[l], ssm_conv_b[l], ssm_dt_bias[l],
                                    ssm_a_log[l], ssm_d[l], ssm_norm[l], ml_conv_w[l], ml_conv_b[l], ml_gate_b[l],
                                    ml_norm[l], s5_a_re[l], s5_a_im[l], s5_log_dt[l], s5_b_re[l], s5_b_im[l],
                                    s5_c_re[l], s5_c_im[l], s5_d[l], s5_glu_w[l], s5_glu_b[l], gdn_conv_w[l],
                                    gdn_dt_bias[l], gdn_a_log[l], gdn_norm[l], gate_b[l], w_branch[l], w_out[l])
        if col_order:
            y_lat = col_to_raster(y_lat, rows)
        h_lat = h_lat + mod[5] * y_lat
        h_lat = h_lat + 0.5 * mod[8] * swiglu(modulate(rmsnorm(h_lat, norm_w[l, 2]), mod[6], mod[7]), ffn_up[l, 1], ffn_down[l, 1])
        if need_ctx:
            h_ctx = h_ctx + mod_c[5] * y_ctx
            h_ctx = h_ctx + 0.5 * mod_c[8] * swiglu(modulate(rmsnorm(h_ctx, norm_w[l, 2]), mod_c[6], mod_c[7]), ffn_up[l, 1], ffn_down[l, 1])
    return rmsnorm(h_lat, final_norm)
```

```python
import functools

import jax
import jax.numpy as jnp
from jax import lax
from jax.experimental import pallas as pl
from jax.experimental.pallas import tpu as pltpu

F32 = jnp.float32
BF16 = jnp.bfloat16

D = 1024
GRID_W = 64
N_MOD = 9
D_FF = 2816
CONV_K = 5
EPS = 1e-6
N_BRANCH = 4
BW = 512
SSM_H, SSM_P, SSM_G, SSM_N, SSM_Q = 8, 64, 2, 128, 128
ML_H, ML_DK, ML_DV, ML_Q = 4, 64, 128, 64
S5_G, S5_GC, S5_P, S5_Q = 32, 16, 64, 16
GDN_H, GDN_DK, GDN_DV, GDN_Q = 4, 128, 128, 64
IN_SPLITS = (512, 1024, 16, 512, 512, 512, 16, 512, 1536, 512, 8, 8, 4096)

LANES = 128
SUBLANES = 8
VMEM_LIMIT = 56 * 1024 * 1024

TM = 256
HALO = SUBLANES
CONV_W = 1024 + 512 + 1536
N_SMALL = 2 * LANES
REST_W = 4 * BW + N_SMALL
REST_COL = dict(ssm_z=0, ml_v=1, ml_o=2, gdn_z=3)
S5_SPLIT = BW // LANES
L_DT, L_IG, L_FG, L_GA, L_GB = 0, 8, 12, 16, 20
S5_SEG = 2048


def _cparams(n_axes):
    return pltpu.CompilerParams(dimension_semantics=("arbitrary",) * n_axes,
                                vmem_limit_bytes=VMEM_LIMIT)


def _const_spec(shape):
    zeros = (0,) * len(shape)
    return pl.BlockSpec(shape, lambda *_: zeros)


def _sigmoid(x):
    return 1.0 / (1.0 + jnp.exp(-x))


def _silu(x):
    return x * _sigmoid(x)


def _softplus(x):
    return jnp.maximum(x, 0.0) + jnp.log1p(jnp.exp(-jnp.abs(x)))


def _gelu_tanh(x):
    return 0.5 * x * (1.0 + jnp.tanh(0.7978845608028654 * (x + 0.044715 * (x * x * x))))


def _rms(x, w):
    return x * lax.rsqrt(jnp.mean(x * x, axis=-1, keepdims=True) + EPS) * w


def _bdot(a, b):
    return jnp.dot(a.astype(BF16), b.astype(BF16), preferred_element_type=F32)


def _bdot_nt(a, b):
    return lax.dot_general(a.astype(BF16), b.astype(BF16), (((1,), (1,)), ((), ())),
                           preferred_element_type=F32)


def _bdot_tn(a, b):
    return lax.dot_general(a.astype(BF16), b.astype(BF16), (((0,), (0,)), ((), ())),
                           preferred_element_type=F32)


def _split3(x):
    hi = x.astype(BF16)
    r = x - hi.astype(F32)
    mid = r.astype(BF16)
    lo = (r - mid.astype(F32)).astype(BF16)
    return hi, mid, lo


def _dot_hi(a, b):
    ah, bh = a.astype(BF16), b.astype(BF16)
    al, bl = (a - ah.astype(F32)).astype(BF16), (b - bh.astype(F32)).astype(BF16)
    dot = functools.partial(jnp.dot, preferred_element_type=F32)
    return dot(ah, bh) + (dot(ah, bl) + dot(al, bh))


def _sel_dot_l(sel, x):
    return sum(jnp.dot(sel, p, preferred_element_type=F32) for p in _split3(x))


def _sel_dot_r(x, sel):
    return sum(jnp.dot(p, sel, preferred_element_type=F32) for p in _split3(x))


def _scan_masks(q, fwd):
    ri = lax.broadcasted_iota(jnp.int32, (q, q), 0)
    ci = lax.broadcasted_iota(jnp.int32, (q, q), 1)
    ahead = (ri - ci) * jnp.where(fwd, 1, -1)
    return ahead >= 0, ahead > 0


def _mod_kernel(c_ref, w_ref, b_ref, o_ref):
    s = _silu(c_ref[...])
    o_ref[0, 0] = _bdot(s, w_ref[0]) + b_ref[0, 0]


def _mod_table(cvec, ada_w, ada_b):
    depth = ada_w.shape[0]
    out = pl.pallas_call(
        _mod_kernel,
        grid=(depth, N_MOD),
        in_specs=[_const_spec((SUBLANES, D)),
                  pl.BlockSpec((1, D, D), lambda l, j: (l, 0, j)),
                  pl.BlockSpec((1, 1, 1, D), lambda l, j: (l, j, 0, 0))],
        out_specs=pl.BlockSpec((1, 1, SUBLANES, D), lambda l, j: (l, j, 0, 0)),
        out_shape=jax.ShapeDtypeStruct((depth, N_MOD, SUBLANES, D), F32),
        compiler_params=_cparams(2),
        name="mod_table",
    )(cvec, ada_w, ada_b.reshape(depth, N_MOD, 1, D))
    return jnp.transpose(out, (0, 2, 1, 3))


def _ffn_kernel(h_ref, mod_ref, nw_ref, wup_ref, wdn_ref, *rest, base, final):
    o_ref = rest[-1]
    x = h_ref[0]
    mod = mod_ref[0]
    xm = _rms(x, nw_ref[...]) * (1.0 + mod[base + 1:base + 2]) + mod[base:base + 1]
    xb = xm.astype(BF16)
    g = jnp.dot(xb, wup_ref[:, :D_FF], preferred_element_type=F32)
    u = jnp.dot(xb, wup_ref[:, D_FF:], preferred_element_type=F32)
    y = jnp.dot((_silu(g) * u).astype(BF16), wdn_ref[...], preferred_element_type=F32)
    out = x + (0.5 * mod[base + 2:base + 3]) * y
    if final:
        out = _rms(out, rest[0][...])
    o_ref[0] = out


def _ffn(h, mod, mod_row, nw, w_up, w_dn, base, final_w=None):
    bsz, n, _ = h.shape
    row = (lambda b: b) if mod_row is None else (lambda b: mod_row)
    in_specs = [pl.BlockSpec((1, TM, D), lambda b, t: (b, t, 0)),
                pl.BlockSpec((1, N_MOD, D), lambda b, t: (row(b), 0, 0)),
                _const_spec((1, D)),
                _const_spec((D, 2 * D_FF)),
                _const_spec((D_FF, D))]
    args = [h, mod, nw.reshape(1, D), w_up, w_dn]
    if final_w is not None:
        in_specs.append(_const_spec((1, D)))
        args.append(final_w.reshape(1, D))
    return pl.pallas_call(
        functools.partial(_ffn_kernel, base=base, final=final_w is not None),
        grid=(bsz, n // TM),
        in_specs=in_specs,
        out_specs=pl.BlockSpec((1, TM, D), lambda b, t: (b, t, 0)),
        out_shape=jax.ShapeDtypeStruct(h.shape, F32),
        compiler_params=_cparams(2),
        name="ffn",
    )(*args)


def _lat_tile_spec(n, col_rows):
    if col_rows is None:
        return (1, TM, D), (lambda a: a)
    wpt = TM // col_rows
    return (1, col_rows, wpt * D), (lambda a: a.reshape(a.shape[0], col_rows, GRID_W * D))


def _tile_rows(blk, col_rows):
    if col_rows is None:
        return blk
    wpt = TM // col_rows
    return jnp.concatenate([blk[:, k * D:(k + 1) * D] for k in range(wpt)], axis=0)


def _tile_unrows(x, col_rows):
    if col_rows is None:
        return x
    wpt = TM // col_rows
    return jnp.concatenate([x[k * col_rows:(k + 1) * col_rows] for k in range(wpt)], axis=1)


def _inproj_kernel(mod_ref, hl_ref, hp_ref, hn_ref, hc_ref, nw_ref, wc_ref, wr_ref, cw_ref, cb_ref,
                   oc_ref, or_ref, ou_ref, *, nt, col_rows):
    t = pl.program_id(1)
    is_ctx = t == nt
    mod = mod_ref[0]

    def modulated(x):
        return _rms(x, nw_ref[...]) * (1.0 + mod[4:5]) + mod[3:4]

    has_prev = jnp.logical_and(jnp.logical_not(is_ctx), t >= 1)
    has_next = jnp.logical_and(jnp.logical_not(is_ctx), t <= nt - 2)
    x_prev = modulated(hp_ref[0])
    x_next = modulated(hn_ref[0])
    x_prev = jnp.where(has_prev, x_prev, jnp.zeros_like(x_prev))
    x_next = jnp.where(has_next, x_next, jnp.zeros_like(x_next))
    x_main = modulated(jnp.where(is_ctx, hc_ref[0], _tile_rows(hl_ref[0], col_rows)))
    xb = jnp.concatenate([x_prev, x_main, x_next], axis=0).astype(BF16)
    pc = jnp.dot(xb, wc_ref[...], preferred_element_type=F32)
    acc = cb_ref[...]
    for k in range(CONV_K):
        off = HALO - CONV_K // 2 + k
        acc = acc + cw_ref[k:k + 1, :] * pc[off:off + TM]
    oc_ref[0] = _silu(acc)
    pr = jnp.dot(x_main.astype(BF16), wr_ref[...], preferred_element_type=F32)
    or_ref[0] = pr[:, :REST_W]
    for k in range(S5_SPLIT):
        ou_ref[0, k] = pr[:, REST_W + k * LANES:REST_W + (k + 1) * LANES]


def _inproj(h_lat, h_ctx, mod, nw, wc, wr, cw, cb, col_rows):
    bsz, n, _ = h_lat.shape
    nt = n // TM
    s_tot = n + TM
    tile_shape, view = _lat_tile_spec(n, col_rows)
    lat_t = lambda t: jnp.minimum(t, nt - 1)
    if col_rows is None:
        hb = TM // HALO
        nb = n // HALO
        tile_map = lambda b, t: (b, lat_t(t), 0)
        prev_map = lambda b, t: (b, jnp.clip(t * hb - 1, 0, nb - 1), 0)
        next_map = lambda b, t: (b, jnp.clip((t + 1) * hb, 0, nb - 1), 0)
    else:
        wpt = TM // col_rows
        tile_map = lambda b, t: (b, 0, lat_t(t))
        prev_map = lambda b, t: (b, col_rows // HALO - 1, jnp.clip(t * wpt - 1, 0, GRID_W - 1))
        next_map = lambda b, t: (b, 0, jnp.clip((t + 1) * wpt, 0, GRID_W - 1))
    hv = view(h_lat)
    return pl.pallas_call(
        functools.partial(_inproj_kernel, nt=nt, col_rows=col_rows),
        grid=(bsz, nt + 1),
        in_specs=[pl.BlockSpec((1, N_MOD, D), lambda b, t: (jnp.where(t == nt, bsz, b), 0, 0)),
                  pl.BlockSpec(tile_shape, tile_map),
                  pl.BlockSpec((1, HALO, D), prev_map),
                  pl.BlockSpec((1, HALO, D), next_map),
                  pl.BlockSpec((1, TM, D), lambda b, t: (b, 0, 0)),
                  _const_spec((1, D)),
                  _const_spec((D, CONV_W)),
                  _const_spec((D, REST_W + BW)),
                  _const_spec((SUBLANES, CONV_W)),
                  _const_spec((1, CONV_W))],
        out_specs=[pl.BlockSpec((1, TM, CONV_W), lambda b, t: (b, t, 0)),
                   pl.BlockSpec((1, TM, REST_W), lambda b, t: (b, t, 0)),
                   pl.BlockSpec((1, S5_SPLIT, TM, LANES), lambda b, t: (b, 0, t, 0))],
        out_shape=[jax.ShapeDtypeStruct((bsz, s_tot, CONV_W), F32),
                   jax.ShapeDtypeStruct((bsz, s_tot, REST_W), F32),
                   jax.ShapeDtypeStruct((bsz, S5_SPLIT, s_tot, LANES), F32)],
        compiler_params=_cparams(2),
        name="inproj",
    )(mod, hv, hv, hv, h_ctx, nw.reshape(1, D), wc, wr, cw, cb)


def _chunk_index(n_lat, q):
    ncl, ncc = n_lat // q, TM // q

    def mem_chunk(d, j):
        ctx = jnp.where(d == 0, ncl + j, ncl + ncc - 1 - j)
        lat = jnp.where(d == 0, j - ncc, ncl - 1 - (j - ncc))
        return jnp.where(j < ncc, ctx, lat)
    return mem_chunk, ncl + ncc


def _ssd_kernel(xbc_ref, sm_ref, par_ref, parx_ref, ex_ref, y_ref, h_ref):
    d = pl.program_id(1)

    @pl.when(pl.program_id(2) == 0)
    def _():
        h_ref[...] = jnp.zeros_like(h_ref)

    q = SSM_Q
    fwd = d == 0
    xbc = xbc_ref[0]
    xs = xbc[:, :BW]
    bm = xbc[:, BW:BW + SSM_G * SSM_N].astype(BF16)
    cm = xbc[:, BW + SSM_G * SSM_N:].astype(BF16)
    par = par_ref[0]
    lane = lax.broadcasted_iota(jnp.int32, (1, LANES), 1)
    dt = jnp.where(lane < SSM_H, _softplus(sm_ref[0] + par[0:1]), 0.0)
    da = dt * (-jnp.exp(par[1:2]))
    incl, _ = _scan_masks(q, fwd)
    sel = incl.astype(F32).astype(BF16)
    g = _sel_dot_l(sel, da)
    g_t = g.T
    dtx = _sel_dot_r(dt, ex_ref[...])
    dax = dtx * (-jnp.exp(parx_ref[0, 0:1]))
    gx = _sel_dot_l(sel, dax)
    glx = jnp.sum(dax, axis=0, keepdims=True)
    xdt = xs * dtx
    xdt_b = xdt.astype(BF16)
    half = (lax.broadcasted_iota(jnp.int32, (1, BW), 1) % (2 * SSM_P)) < SSM_P
    x_even = jnp.where(half, xdt_b, jnp.zeros_like(xdt_b))
    x_odd = jnp.where(half, jnp.zeros_like(xdt_b), xdt_b)
    hpg = SSM_H // SSM_G
    cb = [_bdot_nt(cm[:, k * SSM_N:(k + 1) * SSM_N], bm[:, k * SSM_N:(k + 1) * SSM_N])
          for k in range(SSM_G)]

    def scores(h):
        dec = jnp.exp(jnp.where(incl, g[:, h:h + 1] - g_t[h:h + 1, :], -jnp.inf))
        return (cb[h // hpg] * dec).astype(BF16)

    pairs = []
    for p in range(SSM_H // 2):
        cols = slice(p * 2 * SSM_P, (p + 1) * 2 * SSM_P)
        pairs.append(jnp.dot(scores(2 * p), x_even[:, cols], preferred_element_type=F32)
                     + jnp.dot(scores(2 * p + 1), x_odd[:, cols], preferred_element_type=F32))
    y = jnp.concatenate(pairs, axis=1)
    h_prev = h_ref[...]
    hb = h_prev.astype(BF16)
    gw = hpg * SSM_P
    y_inter = jnp.concatenate(
        [jnp.dot(cm[:, k * SSM_N:(k + 1) * SSM_N], hb[:, k * gw:(k + 1) * gw], preferred_element_type=F32)
         for k in range(SSM_G)], axis=1)
    y = y + y_inter * jnp.exp(gx)
    xw = (xdt * jnp.exp(glx - gx)).astype(BF16)
    st = jnp.concatenate(
        [_bdot_tn(bm[:, k * SSM_N:(k + 1) * SSM_N], xw[:, k * gw:(k + 1) * gw]) for k in range(SSM_G)],
        axis=1)
    h_ref[...] = jnp.exp(glx) * h_prev + st
    skip = jnp.where(fwd, parx_ref[0, 1:2], 0.0)
    y_ref[0, 0] = y + skip * xs


def _ssd(conv_out, rest_out, mixpar, ssm_parx, expand, n_lat):
    bsz, s_tot, _ = conv_out.shape
    mem_chunk, nc = _chunk_index(n_lat, SSM_Q)
    sm_blk = (REST_W - N_SMALL) // LANES
    return pl.pallas_call(
        _ssd_kernel,
        grid=(bsz, 2, nc),
        in_specs=[pl.BlockSpec((1, SSM_Q, 2 * BW), lambda b, d, j: (b, mem_chunk(d, j), 0)),
                  pl.BlockSpec((1, SSM_Q, LANES), lambda b, d, j: (b, mem_chunk(d, j), sm_blk + d)),
                  pl.BlockSpec((1, SUBLANES, LANES), lambda b, d, j: (d, 0, 0)),
                  pl.BlockSpec((1, SUBLANES, BW), lambda b, d, j: (d, 0, 0)),
                  _const_spec((LANES, BW))],
        out_specs=pl.BlockSpec((1, 1, SSM_Q, BW), lambda b, d, j: (d, b, mem_chunk(d, j), 0)),
        out_shape=jax.ShapeDtypeStruct((2, bsz, s_tot, BW), F32),
        scratch_shapes=[pltpu.VMEM((SSM_N, BW), F32)],
        compiler_params=_cparams(3),
        name="ssd",
    )(conv_out, rest_out, mixpar, ssm_parx, expand)


def _mlstm_kernel(qk_ref, v_ref, sm_ref, par_ref, y_ref, c_ref, m_ref):
    d = pl.program_id(1)

    @pl.when(pl.program_id(2) == 0)
    def _():
        c_ref[...] = jnp.zeros_like(c_ref)
        m_ref[...] = jnp.zeros_like(m_ref)

    q = ML_Q
    fwd = d == 0
    qk = qk_ref[0]
    v = v_ref[0]
    pre = sm_ref[0] + par_ref[0, 0:1]
    lf = -_softplus(-pre)
    incl, _ = _scan_masks(q, fwd)
    sel = incl.astype(F32).astype(BF16)
    bt = _sel_dot_l(sel, lf)
    pre_t = pre.T
    bt_t = bt.T
    tot = jnp.sum(lf, axis=0, keepdims=True)
    ones = jnp.ones((q, ML_DV), F32)
    outs = []
    for h in range(ML_H):
        qh = qk[:, h * ML_DK:(h + 1) * ML_DK] * (ML_DK ** -0.5)
        kh = qk[:, ML_H * ML_DK + h * ML_DK:ML_H * ML_DK + (h + 1) * ML_DK]
        vext = jnp.concatenate([v[:, h * ML_DV:(h + 1) * ML_DV], ones], axis=1).astype(BF16)
        ig_c, ig_r = pre[:, L_IG + h:L_IG + h + 1], pre_t[L_IG + h:L_IG + h + 1, :]
        bt_c, bt_r = bt[:, L_FG + h:L_FG + h + 1], bt_t[L_FG + h:L_FG + h + 1, :]
        bl = tot[:, L_FG + h:L_FG + h + 1]
        dmat = jnp.where(incl, bt_c - bt_r, -jnp.inf) + ig_r
        m_intra = jnp.max(dmat, axis=1, keepdims=True)
        m_loc = jnp.max(bl - bt_r + ig_r, axis=1, keepdims=True)
        w_end = jnp.exp(bl - bt_c + ig_c - m_loc)
        c_loc = _bdot_tn(kh * w_end, vext)
        c_prev = c_ref[h]
        m_prev = m_ref[h:h + 1, 0:1]
        e = bt_c + m_prev
        m_out = jnp.maximum(e, m_intra)
        w_intra = jnp.exp(dmat - m_out) * _bdot_nt(qh, kh)
        w_x = jnp.exp(e - m_out)
        qc = _bdot(qh, c_prev)
        num = _bdot(w_intra, vext[:, :ML_DV]) + w_x * qc[:, :ML_DV]
        den = jnp.sum(w_intra, axis=1, keepdims=True) + w_x * qc[:, ML_DV:ML_DV + 1]
        den = jnp.maximum(jnp.abs(den), jnp.exp(-m_out))
        outs.append(num / den)
        m_new = jnp.maximum(bl + m_prev, m_loc)
        c_ref[h] = jnp.exp(bl + m_prev - m_new) * c_prev + jnp.exp(m_loc - m_new) * c_loc
        m_ref[h:h + 1, :] = jnp.broadcast_to(m_new, (1, LANES))
    y_ref[0, 0] = jnp.concatenate(outs, axis=1)


def _mlstm(conv_out, rest_out, mixpar, n_lat):
    bsz, s_tot, _ = conv_out.shape
    mem_chunk, nc = _chunk_index(n_lat, ML_Q)
    sm_blk = (REST_W - N_SMALL) // LANES
    return pl.pallas_call(
        _mlstm_kernel,
        grid=(bsz, 2, nc),
        in_specs=[pl.BlockSpec((1, ML_Q, BW), lambda b, d, j: (b, mem_chunk(d, j), 2)),
                  pl.BlockSpec((1, ML_Q, BW), lambda b, d, j: (b, mem_chunk(d, j), 1)),
                  pl.BlockSpec((1, ML_Q, LANES), lambda b, d, j: (b, mem_chunk(d, j), sm_blk + d)),
                  pl.BlockSpec((1, SUBLANES, LANES), lambda b, d, j: (d, 0, 0))],
        out_specs=pl.BlockSpec((1, 1, ML_Q, BW), lambda b, d, j: (d, b, mem_chunk(d, j), 0)),
        out_shape=jax.ShapeDtypeStruct((2, bsz, s_tot, BW), F32),
        scratch_shapes=[pltpu.VMEM((ML_H, ML_DK, 2 * ML_DV), F32),
                        pltpu.VMEM((SUBLANES, LANES), F32)],
        compiler_params=_cparams(3),
        name="mlstm",
    )(conv_out, rest_out, rest_out, mixpar)


def _unit_lower_inverse(l_strict):
    q = l_strict.shape[0]
    eye = (lax.broadcasted_iota(jnp.int32, (q, q), 0) == lax.broadcasted_iota(jnp.int32, (q, q), 1))
    p = -l_strict
    t = eye.astype(F32) + p
    steps = max(q.bit_length() - 2, 0)
    for _ in range(steps):
        p = _dot_hi(p, p)
        t = t + _dot_hi(t, p)
    return t


def _l2norm(a):
    return a * lax.rsqrt(jnp.sum(a * a, axis=-1, keepdims=True) + EPS)


def _gdn_kernel(qkv_ref, sm_ref, par_ref, y_ref, s_ref):
    d = pl.program_id(1)

    @pl.when(pl.program_id(2) == 0)
    def _():
        s_ref[...] = jnp.zeros_like(s_ref)

    q = GDN_Q
    fwd = d == 0
    qkv = qkv_ref[0]
    sm = sm_ref[0]
    par = par_ref[0]
    lg = -jnp.exp(par[1:2]) * _softplus(sm + par[0:1])
    beta = _sigmoid(sm)
    incl, strict = _scan_masks(q, fwd)
    g = _sel_dot_l(incl.astype(F32).astype(BF16), lg)
    g_t = g.T
    tot = jnp.sum(lg, axis=0, keepdims=True)
    hk = GDN_H * GDN_DK
    outs = []
    for h in range(GDN_H):
        qh = _l2norm(qkv[:, h * GDN_DK:(h + 1) * GDN_DK]) * (GDN_DK ** -0.5)
        kh = _l2norm(qkv[:, hk + h * GDN_DK:hk + (h + 1) * GDN_DK])
        vh = qkv[:, 2 * hk + h * GDN_DV:2 * hk + (h + 1) * GDN_DV]
        b_c = beta[:, L_GB + h:L_GB + h + 1]
        g_c, g_r = g[:, L_GA + h:L_GA + h + 1], g_t[L_GA + h:L_GA + h + 1, :]
        gl = tot[:, L_GA + h:L_GA + h + 1]
        diff = g_c - g_r
        kb = kh * b_c
        t_inv = _unit_lower_inverse(_bdot_nt(kb, kh) * jnp.exp(jnp.where(strict, diff, -jnp.inf)))
        eg = jnp.exp(g_c)
        uw = _bdot(t_inv, jnp.concatenate([vh * b_c, kb * eg], axis=1))
        a_qk = _bdot_nt(qh, kh) * jnp.exp(jnp.where(incl, diff, -jnp.inf))
        s_prev = s_ref[h]
        ws = _bdot(jnp.concatenate([uw[:, GDN_DV:], qh * eg], axis=0), s_prev)
        u = uw[:, :GDN_DV] - ws[:q]
        outs.append(ws[q:] + _bdot(a_qk, u))
        s_ref[h] = jnp.exp(gl) * s_prev + _bdot_tn(kh * jnp.exp(gl - g_c), u)
    y_ref[0, 0] = jnp.concatenate(outs, axis=1)


def _gdn(conv_out, rest_out, mixpar, n_lat):
    bsz, s_tot, _ = conv_out.shape
    mem_chunk, nc = _chunk_index(n_lat, GDN_Q)
    sm_blk = (REST_W - N_SMALL) // LANES
    return pl.pallas_call(
        _gdn_kernel,
        grid=(bsz, 2, nc),
        in_specs=[pl.BlockSpec((1, GDN_Q, 3 * BW), lambda b, d, j: (b, mem_chunk(d, j), 1)),
                  pl.BlockSpec((1, GDN_Q, LANES), lambda b, d, j: (b, mem_chunk(d, j), sm_blk + d)),
                  pl.BlockSpec((1, SUBLANES, LANES), lambda b, d, j: (d, 0, 0))],
        out_specs=pl.BlockSpec((1, 1, GDN_Q, BW), lambda b, d, j: (d, b, mem_chunk(d, j), 0)),
        out_shape=jax.ShapeDtypeStruct((2, bsz, s_tot, BW), F32),
        scratch_shapes=[pltpu.VMEM((GDN_H, GDN_DK, GDN_DV), F32)],
        compiler_params=_cparams(3),
        name="gdn",
    )(conv_out, rest_out, mixpar)


def _s5_scan(sloc, h_init, pw, fwd, n_valid):
    n = sloc.shape[0]
    row = lax.broadcasted_iota(jnp.int32, (n, 1), 0)

    def mul(x, i):
        return pw[2 * i:2 * i + 1] * x + pw[2 * i + 1:2 * i + 2] * pltpu.roll(x, S5_P, axis=1)

    pos = jnp.where(fwd, row, n - 1 - row)
    first = jnp.where(fwd, 0, n_valid - 1)
    z = pltpu.roll(sloc, jnp.where(fwd, 1, n - 1), axis=0)
    z = jnp.where(pos >= jnp.where(fwd, 1, n - n_valid + 1), z, 0.0)
    z = jnp.where(row == first, h_init, z)
    s, i = 1, 0
    while s < n:
        zs = pltpu.roll(z, jnp.where(fwd, s, n - s), axis=0)
        zs = jnp.where(pos >= s, zs, 0.0)
        z = z + mul(zs, i)
        s, i = 2 * s, i + 1
    last = jnp.where(fwd, n_valid - 1, 0)
    pick = row == last
    z_last = jnp.sum(jnp.where(pick, z, 0.0), axis=0, keepdims=True)
    s_last = jnp.sum(jnp.where(pick, sloc, 0.0), axis=0, keepdims=True)
    return z, mul(z_last, 0) + s_last


def _s5_segment(load_rows, store_rows, n_rows, n_valid, fwd, ut_ref, st_ref, wcat_ref, pm_ref, pw_ref,
                skip_row):
    qg = S5_Q * S5_GC
    for t in range(S5_Q):
        vt = load_rows(t)
        ut_ref[:, t * S5_GC:(t + 1) * S5_GC, :] = vt.T.reshape(S5_G, S5_GC, n_rows)

    def group(gi, carry):
        ug = ut_ref[gi].T.astype(BF16)
        y1 = jnp.dot(ug, wcat_ref[0, gi], preferred_element_type=F32)
        h_start, h_end = _s5_scan(y1[:, qg:], st_ref[gi, 0:1, :], pw_ref[0, gi], fwd, n_valid)
        st_ref[gi, 0:1, :] = h_end
        yy = y1[:, :qg] + jnp.dot(h_start.astype(BF16), pm_ref[0, gi], preferred_element_type=F32)
        ut_ref[gi] = yy.T
        return carry

    lax.fori_loop(0, S5_G, group, 0)
    for t in range(S5_Q):
        wt = ut_ref[:, t * S5_GC:(t + 1) * S5_GC, :].reshape(BW, n_rows).T
        store_rows(t, wt + skip_row * load_rows(t))


def _load_token(u_ref, t, n_chunks):
    return jnp.concatenate([u_ref[0, k, pl.ds(t, n_chunks, stride=S5_Q), :] for k in range(S5_SPLIT)], axis=1)


def _store_token(y_ref, t, n_chunks, val):
    for k in range(S5_SPLIT):
        y_ref[0, 0, k, pl.ds(t, n_chunks, stride=S5_Q), :] = val[:, k * LANES:(k + 1) * LANES]


def _s5_kernel(ul_ref, uc_ref, wcat_ref, pm_ref, pw_ref, dsk_ref, yl_ref, yc_ref, ut_ref, utc_ref, st_ref,
               *, n_seg_chunks):
    d = pl.program_id(1)
    fwd = d == 0
    skip_row = jnp.where(fwd, dsk_ref[...], 0.0)
    ncc = TM // S5_Q

    @pl.when(pl.program_id(2) == 0)
    def _():
        st_ref[...] = jnp.zeros_like(st_ref)
        pad = jnp.zeros((LANES - ncc, BW), F32)

        def load_c(t):
            return jnp.concatenate([_load_token(uc_ref, t, ncc), pad], axis=0)

        def store_c(t, val):
            _store_token(yc_ref, t, ncc, val[:ncc])

        _s5_segment(load_c, store_c, LANES, ncc, fwd, utc_ref, st_ref, wcat_ref, pm_ref, pw_ref, skip_row)

    def load_l(t):
        return _load_token(ul_ref, t, n_seg_chunks)

    def store_l(t, val):
        _store_token(yl_ref, t, n_seg_chunks, val)

    _s5_segment(load_l, store_l, n_seg_chunks, n_seg_chunks, fwd, ut_ref, st_ref, wcat_ref, pm_ref, pw_ref,
                skip_row)


def _s5(u_slabs, wcat, pm, pw, dskip, n_lat):
    bsz = u_slabs.shape[0]
    seg = min(S5_SEG, n_lat)
    nseg = n_lat // seg
    nsc = seg // S5_Q
    qg = S5_Q * S5_GC
    seg_mem = lambda d, s: jnp.where(d == 0, s, nseg - 1 - s)
    ctx_blk = n_lat // TM
    yl, yc = pl.pallas_call(
        functools.partial(_s5_kernel, n_seg_chunks=nsc),
        grid=(bsz, 2, nseg),
        in_specs=[pl.BlockSpec((1, S5_SPLIT, seg, LANES), lambda b, d, s: (b, 0, seg_mem(d, s), 0)),
                  pl.BlockSpec((1, S5_SPLIT, TM, LANES), lambda b, d, s: (b, 0, ctx_blk, 0)),
                  pl.BlockSpec((1, S5_G, qg, qg + 2 * S5_P), lambda b, d, s: (d, 0, 0, 0)),
                  pl.BlockSpec((1, S5_G, 2 * S5_P, qg), lambda b, d, s: (d, 0, 0, 0)),
                  pl.BlockSpec((1, S5_G, 2 * SUBLANES, LANES), lambda b, d, s: (d, 0, 0, 0)),
                  _const_spec((1, BW))],
        out_specs=[pl.BlockSpec((1, 1, S5_SPLIT, seg, LANES), lambda b, d, s: (d, b, 0, seg_mem(d, s), 0)),
                   pl.BlockSpec((1, 1, S5_SPLIT, TM, LANES), lambda b, d, s: (d, b, 0, 0, 0))],
        out_shape=[jax.ShapeDtypeStruct((2, bsz, S5_SPLIT, n_lat, LANES), F32),
                   jax.ShapeDtypeStruct((2, bsz, S5_SPLIT, TM, LANES), F32)],
        scratch_shapes=[pltpu.VMEM((S5_G, qg, nsc), F32),
                        pltpu.VMEM((S5_G, qg, LANES), F32),
                        pltpu.VMEM((S5_G, SUBLANES, LANES), F32)],
        compiler_params=_cparams(3),
        name="s5",
    )(u_slabs, u_slabs, wcat, pm, pw, dskip)
    return yl, yc


def _s5_operators(a_re, a_im, log_dt, b_re, b_im, c_re, c_im):
    hi = lax.Precision.HIGHEST
    q, nq = S5_Q, S5_Q + 1
    lre = jnp.minimum(a_re.astype(F32), -1e-4)
    lim = a_im.astype(F32)
    dt = jnp.exp(log_dt.astype(F32))[..., None]
    mag = jnp.exp(lre * dt)
    ab_re, ab_im = mag * jnp.cos(lim * dt), mag * jnp.sin(lim * dt)
    den = lre * lre + lim * lim
    f_re = ((ab_re - 1.0) * lre + ab_im * lim) / den
    f_im = (ab_im * lre - (ab_re - 1.0) * lim) / den
    br, bi = b_re.astype(F32), b_im.astype(F32)
    bb_re = f_re[..., None] * br - f_im[..., None] * bi
    bb_im = f_re[..., None] * bi + f_im[..., None] * br
    pr, pi = [jnp.ones_like(ab_re)], [jnp.zeros_like(ab_im)]
    for _ in range(q):
        pr, pi = pr + [pr[-1] * ab_re - pi[-1] * ab_im], pi + [pr[-1] * ab_im + pi[-1] * ab_re]
    pw_re, pw_im = jnp.stack(pr, axis=2), jnp.stack(pi, axis=2)
    e_re = pw_re[..., None] * bb_re[:, :, None] - pw_im[..., None] * bb_im[:, :, None]
    e_im = pw_re[..., None] * bb_im[:, :, None] + pw_im[..., None] * bb_re[:, :, None]
    cr, ci = c_re.astype(F32), c_im.astype(F32)
    kern = (jnp.einsum('gcp,dgnpe->dgnce', cr, e_re, precision=hi)
            - jnp.einsum('gcp,dgnpe->dgnce', ci, e_im, precision=hi))
    s_idx = jnp.arange(q)[:, None]
    t_idx = jnp.arange(q)[None, :]
    mats, rmats, pmats = [], [], []
    for d in range(2):
        lag = (t_idx - s_idx) if d == 0 else (s_idx - t_idx)
        valid = lag >= 0
        m = kern[d][:, jnp.clip(lag, 0, q)]
        m = jnp.where(valid[None, :, :, None, None], m, 0.0)
        mats.append(jnp.transpose(m, (0, 1, 4, 2, 3)).reshape(S5_G, q * S5_GC, q * S5_GC))
        e_s = (q - 1 - jnp.arange(q)) if d == 0 else jnp.arange(q)
        r = jnp.concatenate([e_re[d][:, e_s], e_im[d][:, e_s]], axis=2)
        rmats.append(jnp.transpose(r, (0, 1, 3, 2)).reshape(S5_G, q * S5_GC, 2 * S5_P))
        f_t = (jnp.arange(q) + 1) if d == 0 else (q - jnp.arange(q))
        fr, fi = pw_re[d][:, f_t], pw_im[d][:, f_t]
        p_from_re = cr[:, None] * fr[:, :, None] - ci[:, None] * fi[:, :, None]
        p_from_im = -cr[:, None] * fi[:, :, None] - ci[:, None] * fr[:, :, None]
        p = jnp.concatenate([p_from_re, p_from_im], axis=3)
        pmats.append(jnp.transpose(p, (0, 3, 1, 2)).reshape(S5_G, 2 * S5_P, q * S5_GC))
    wcat = jnp.concatenate([jnp.stack(mats), jnp.stack(rmats)], axis=3).astype(BF16)
    pm = jnp.stack(pmats).astype(BF16)
    dr, di = pw_re[:, :, q], pw_im[:, :, q]
    rows = []
    for _ in range(SUBLANES):
        rows += [jnp.concatenate([dr, dr], axis=-1), jnp.concatenate([-di, di], axis=-1)]
        dr, di = dr * dr - di * di, 2.0 * dr * di
    return wcat, pm, jnp.stack(rows, axis=2)


def _head_rms(x, w, width):
    parts = []
    for h in range(x.shape[1] // width):
        seg = x[:, h * width:(h + 1) * width]
        parts.append(seg * lax.rsqrt(jnp.mean(seg * seg, axis=-1, keepdims=True) + EPS))
    return jnp.concatenate(parts, axis=1) * w


def _merge_kernel(h_ref, mod_ref, nw_ref, wg_ref, gb_ref, ya_ref, yb_ref, yc_ref, yd_ref, za_ref, ob_ref,
                  zd_ref, na_ref, nb_ref, nd_ref, glw_ref, glb_ref, wbr_ref, wo_ref, o_ref, *, col_rows):
    x = _tile_rows(h_ref[0], col_rows)
    mod = mod_ref[0]
    xb = (_rms(x, nw_ref[...]) * (1.0 + mod[4:5]) + mod[3:4]).astype(BF16)
    gates = _sigmoid(jnp.dot(xb, wg_ref[...], preferred_element_type=F32) + gb_ref[...])
    ya = _rms((ya_ref[0, 0] + ya_ref[1, 0]) * _silu(za_ref[0]), na_ref[...])
    yb = _sigmoid(ob_ref[0]) * _head_rms(yb_ref[0, 0] + yb_ref[1, 0], nb_ref[...], ML_DV)
    gc = _gelu_tanh(jnp.concatenate([yc_ref[0, 0, k] + yc_ref[1, 0, k] for k in range(S5_SPLIT)], axis=1))
    yc = gc * _sigmoid(_bdot(gc, glw_ref[...]) + glb_ref[...])
    yd = _head_rms(yd_ref[0, 0] + yd_ref[1, 0], nd_ref[...], GDN_DV) * _silu(zd_ref[0])
    merged = jnp.zeros((TM, D), F32)
    for k, yk in enumerate((ya, yb, yc, yd)):
        merged = merged + gates[:, k * D:(k + 1) * D] * _bdot(yk, wbr_ref[k])
    out = _bdot(merged, wo_ref[...])
    o_ref[0] = _tile_unrows(x + mod[5:6] * out, col_rows)


def _merge(h, mod, mod_row, tile0, nw, wg, gb, ys, rest_out, norms, glw, glb, wbr, wo, col_rows):
    bsz, n, _ = h.shape
    nt = n // TM
    tile_shape, view = _lat_tile_spec(n, col_rows)
    tile_map = (lambda b, t: (b, t, 0)) if col_rows is None else (lambda b, t: (b, 0, t))
    row = (lambda b: b) if mod_row is None else (lambda b: mod_row)
    y_spec = lambda t0: pl.BlockSpec((2, 1, TM, BW), lambda b, t: (0, b, t0 + t, 0))
    s5_spec = lambda t0: pl.BlockSpec((2, 1, S5_SPLIT, TM, LANES), lambda b, t: (0, b, 0, t0 + t, 0))
    specs = (y_spec, y_spec, s5_spec, y_spec)
    r_spec = lambda c: pl.BlockSpec((1, TM, BW), lambda b, t: (b, tile0 + t, c))
    hv = view(h)
    out = pl.pallas_call(
        functools.partial(_merge_kernel, col_rows=col_rows),
        grid=(bsz, nt),
        in_specs=[pl.BlockSpec(tile_shape, tile_map),
                  pl.BlockSpec((1, N_MOD, D), lambda b, t: (row(b), 0, 0)),
                  _const_spec((1, D)),
                  _const_spec((D, N_BRANCH * D)),
                  _const_spec((1, N_BRANCH * D)),
                  *[mk(t0) for mk, (_, t0) in zip(specs, ys)],
                  r_spec(REST_COL['ssm_z']), r_spec(REST_COL['ml_o']), r_spec(REST_COL['gdn_z']),
                  _const_spec((1, BW)), _const_spec((1, BW)), _const_spec((1, BW)),
                  _const_spec((BW, BW)), _const_spec((1, BW)),
                  _const_spec((N_BRANCH, BW, D)),
                  _const_spec((D, D))],
        out_specs=pl.BlockSpec(tile_shape, tile_map),
        out_shape=jax.ShapeDtypeStruct(hv.shape, F32),
        compiler_params=_cparams(2),
        name="merge",
    )(hv, mod, nw.reshape(1, D), wg, gb, *[a for a, _ in ys], rest_out, rest_out, rest_out, *norms, glw, glb, wbr, wo)
    return out.reshape(h.shape)


def _layer_params(l, p):
    offs = [0]
    for w in IN_SPLITS:
        offs.append(offs[-1] + w)
    w_in = p['w_in'][l]
    col = lambda i: w_in[:, offs[i]:offs[i + 1]]
    z128 = lambda k: jnp.zeros((D, k), F32)

    def small(d):
        dt = col(2).reshape(D, 2, SSM_H)[:, d]
        gates = col(6).reshape(D, 2, 2 * ML_H)[:, d]
        ga = col(10).reshape(D, 2, GDN_H)[:, d]
        gb = col(11).reshape(D, 2, GDN_H)[:, d]
        used = SSM_H + 2 * ML_H + 2 * GDN_H
        return jnp.concatenate([dt, gates, ga, gb, z128(LANES - used)], axis=1)

    wc = jnp.concatenate([col(1), col(3), col(8)], axis=1).astype(BF16)
    wr = jnp.concatenate([col(0), col(4), col(5), col(9), small(0), small(1), col(7)], axis=1).astype(BF16)
    cw = jnp.concatenate([p['ssm_conv_w'][l], p['ml_conv_w'][l], p['gdn_conv_w'][l]], axis=1)
    cw = jnp.concatenate([cw, jnp.zeros((SUBLANES - CONV_K, CONV_W), F32)], axis=0)
    cb = jnp.concatenate([p['ssm_conv_b'][l], p['ml_conv_b'][l], jnp.zeros((GDN_H * (2 * GDN_DK + GDN_DV),), F32)])

    def par_rows(d):
        pad = jnp.zeros((LANES - L_GB,), F32)
        bias = jnp.concatenate([p['ssm_dt_bias'][l, d], p['ml_gate_b'][l, d].reshape(-1),
                                p['gdn_dt_bias'][l, d], pad])
        alog = jnp.concatenate([p['ssm_a_log'][l, d], jnp.zeros((2 * ML_H,), F32), p['gdn_a_log'][l, d], pad])
        return jnp.concatenate([bias[None], alog[None], jnp.zeros((SUBLANES - 2, LANES), F32)], axis=0)

    mixpar = jnp.stack([par_rows(0), par_rows(1)])

    def parx_rows(d):
        rows = jnp.stack([jnp.repeat(p['ssm_a_log'][l, d], SSM_P), jnp.repeat(p['ssm_d'][l], SSM_P)])
        return jnp.concatenate([rows, jnp.zeros((SUBLANES - 2, BW), F32)], axis=0)

    ssm_parx = jnp.stack([parx_rows(0), parx_rows(1)])
    wcat, pm, pw = _s5_operators(p['s5_a_re'][l], p['s5_a_im'][l], p['s5_log_dt'][l], p['s5_b_re'][l],
                                 p['s5_b_im'][l], p['s5_c_re'][l], p['s5_c_im'][l])
    return dict(
        wc=wc, wr=wr, cw=cw, cb=cb.reshape(1, CONV_W), mixpar=mixpar, ssm_parx=ssm_parx,
        s5_wcat=wcat, s5_pm=pm, s5_pw=pw, s5_d=p['s5_d'][l].reshape(1, BW),
        wg=col(12).astype(BF16), gb=p['gate_b'][l].reshape(1, N_BRANCH * D),
        norms=(p['ssm_norm'][l].reshape(1, BW), p['ml_norm'][l].reshape(1, BW),
               jnp.tile(p['gdn_norm'][l], GDN_H).reshape(1, BW)),
        glw=p['s5_glu_w'][l].astype(BF16), glb=p['s5_glu_b'][l].reshape(1, BW),
        wbr=p['w_branch'][l].astype(BF16), wo=p['w_out'][l].astype(BF16),
        ffn_up=p['ffn_up'][l].astype(BF16), ffn_dn=p['ffn_down'][l].astype(BF16))


def kernel(x, c, ctx, c_ctx, ada_w, ada_b, norm_w, ffn_up, ffn_down, w_in, ssm_conv_w, ssm_conv_b, ssm_dt_bias, ssm_a_log, ssm_d, ssm_norm, ml_conv_w, ml_conv_b, ml_gate_b, ml_norm, s5_a_re, s5_a_im, s5_log_dt, s5_b_re, s5_b_im, s5_c_re, s5_c_im, s5_d, s5_glu_w, s5_glu_b, gdn_conv_w, gdn_dt_bias, gdn_a_log, gdn_norm, gate_b, w_branch, w_out, final_norm):
    p = dict(ffn_up=ffn_up, ffn_down=ffn_down, w_in=w_in, ssm_conv_w=ssm_conv_w, ssm_conv_b=ssm_conv_b,
             ssm_dt_bias=ssm_dt_bias, ssm_a_log=ssm_a_log, ssm_d=ssm_d, ssm_norm=ssm_norm, ml_conv_w=ml_conv_w,
             ml_conv_b=ml_conv_b, ml_gate_b=ml_gate_b, ml_norm=ml_norm, s5_a_re=s5_a_re, s5_a_im=s5_a_im,
             s5_log_dt=s5_log_dt, s5_b_re=s5_b_re, s5_b_im=s5_b_im, s5_c_re=s5_c_re, s5_c_im=s5_c_im, s5_d=s5_d,
             s5_glu_w=s5_glu_w, s5_glu_b=s5_glu_b, gdn_conv_w=gdn_conv_w, gdn_dt_bias=gdn_dt_bias,
             gdn_a_log=gdn_a_log, gdn_norm=gdn_norm, gate_b=gate_b, w_branch=w_branch, w_out=w_out)
    bsz, n, _ = x.shape
    depth = ada_w.shape[0]
    rows = n // GRID_W
    assert ctx.shape[1] == TM and n % TM == 0 and bsz + 1 <= SUBLANES
    assert TM % rows == 0 and rows % HALO == 0 and n % min(S5_SEG, n) == 0
    cvec = jnp.concatenate([c, c_ctx[None], jnp.zeros((SUBLANES - bsz - 1, D), F32)], axis=0)
    modtab = _mod_table(cvec, ada_w, ada_b)
    lane_head = jnp.arange(LANES)[:, None] == (jnp.arange(BW)[None, :] // SSM_P)
    expand = lane_head.astype(BF16)
    h_lat, h_ctx = x, ctx
    nt = n // TM
    for l in range(depth):
        lp = _layer_params(l, p)
        mod = modtab[l]
        need_ctx = l < depth - 1
        col_rows = rows if l % 2 == 1 else None
        h_lat = _ffn(h_lat, mod, None, norm_w[l, 0], lp['ffn_up'][0], lp['ffn_dn'][0], 0)
        h_ctx = _ffn(h_ctx, mod, bsz, norm_w[l, 0], lp['ffn_up'][0], lp['ffn_dn'][0], 0)
        conv_out, rest_out, u_s5 = _inproj(h_lat, h_ctx, mod, norm_w[l, 1], lp['wc'], lp['wr'], lp['cw'],
                                           lp['cb'], col_rows)
        y_ssd = _ssd(conv_out, rest_out, lp['mixpar'], lp['ssm_parx'], expand, n)
        y_ml = _mlstm(conv_out, rest_out, lp['mixpar'], n)
        y_s5_lat, y_s5_ctx = _s5(u_s5, lp['s5_wcat'], lp['s5_pm'], lp['s5_pw'], lp['s5_d'], n)
        y_gdn = _gdn(conv_out, rest_out, lp['mixpar'], n)
        tail = (rest_out, lp['norms'], lp['glw'], lp['glb'], lp['wbr'], lp['wo'])
        head = (norm_w[l, 1], lp['wg'], lp['gb'])
        ys_lat = ((y_ssd, 0), (y_ml, 0), (y_s5_lat, 0), (y_gdn, 0))
        ys_ctx = ((y_ssd, nt), (y_ml, nt), (y_s5_ctx, 0), (y_gdn, nt))
        h_lat = _merge(h_lat, mod, None, 0, *head, ys_lat, *tail, col_rows)
        last = l == depth - 1
        h_lat = _ffn(h_lat, mod, None, norm_w[l, 2], lp['ffn_up'][1], lp['ffn_dn'][1], 6,
                     final_w=final_norm if last else None)
        if need_ctx:
            h_ctx = _merge(h_ctx, mod, bsz, nt, *head, ys_ctx, *tail, None)
            h_ctx = _ffn(h_ctx, mod, bsz, norm_w[l, 2], lp['ffn_up'][1], lp['ffn_dn'][1], 6)
    return h_lat
```

```python
import functools

import jax
import jax.numpy as jnp
from jax import lax
from jax.experimental import pallas as pl
from jax.experimental.pallas import tpu as pltpu

F32 = jnp.float32
BF16 = jnp.bfloat16

D = 1024
GRID_W = 64
N_MOD = 9
D_FF = 2816
CONV_K = 5
EPS = 1e-6
N_BRANCH = 4
BW = 512
SSM_H, SSM_P, SSM_G, SSM_N, SSM_Q = 8, 64, 2, 128, 128
ML_H, ML_DK, ML_DV, ML_Q = 4, 64, 128, 64
S5_G, S5_GC, S5_P, S5_Q = 32, 16, 64, 16
GDN_H, GDN_DK, GDN_DV, GDN_Q = 4, 128, 128, 64
IN_SPLITS = (512, 1024, 16, 512, 512, 512, 16, 512, 1536, 512, 8, 8, 4096)

LANES = 128
SUBLANES = 8
VMEM_LIMIT = 56 * 1024 * 1024

TM = 256
HALO = SUBLANES
CONV_W = 1024 + 512 + 1536
N_SMALL = 2 * LANES
REST_W = 4 * BW + N_SMALL
REST_COL = dict(ssm_z=0, ml_v=1, ml_o=2, gdn_z=3)
S5_SPLIT = BW // LANES
L_DT, L_IG, L_FG, L_GA, L_GB = 0, 8, 12, 16, 20
S5_SEG = 2048


def _cparams(n_axes):
    return pltpu.CompilerParams(dimension_semantics=("arbitrary",) * n_axes,
                                vmem_limit_bytes=VMEM_LIMIT)


def _const_spec(shape):
    zeros = (0,) * len(shape)
    return pl.BlockSpec(shape, lambda *_: zeros)


def _sigmoid(x):
    return 1.0 / (1.0 + jnp.exp(-x))


def _silu(x):
    return x * _sigmoid(x)


def _softplus(x):
    return jnp.maximum(x, 0.0) + jnp.log1p(jnp.exp(-jnp.abs(x)))


def _gelu_tanh(x):
    return 0.5 * x * (1.0 + jnp.tanh(0.7978845608028654 * (x + 0.044715 * (x * x * x))))


def _rms(x, w):
    return x * lax.rsqrt(jnp.mean(x * x, axis=-1, keepdims=True) + EPS) * w


def _bdot(a, b):
    return jnp.dot(a.astype(BF16), b.astype(BF16), preferred_element_type=F32)


def _bdot_nt(a, b):
    return lax.dot_general(a.astype(BF16), b.astype(BF16), (((1,), (1,)), ((), ())),
                           preferred_element_type=F32)


def _bdot_tn(a, b):
    return lax.dot_general(a.astype(BF16), b.astype(BF16), (((0,), (0,)), ((), ())),
                           preferred_element_type=F32)


def _split3(x):
    hi = x.astype(BF16)
    r = x - hi.astype(F32)
    mid = r.astype(BF16)
    lo = (r - mid.astype(F32)).astype(BF16)
    return hi, mid, lo


def _scan_sel3(q, fwd, all_ones=False):
    ri = lax.broadcasted_iota(jnp.int32, (q, 3 * LANES), 0)
    cj = lax.broadcasted_iota(jnp.int32, (q, 3 * LANES), 1) & (LANES - 1)
    ahead = jnp.zeros_like(ri) if all_ones else (ri - cj) * jnp.where(fwd, 1, -1)
    return jnp.where(jnp.where(cj < q, ahead, -1) >= 0, 1.0, 0.0).astype(BF16)


def _sel_dot_l(sel3, x):
    rows = []
    for p in _split3(x):
        rows.append(p)
        if p.shape[0] < LANES:
            rows.append(jnp.zeros((LANES - p.shape[0], p.shape[1]), BF16))
    return jnp.dot(sel3, jnp.concatenate(rows, axis=0), preferred_element_type=F32)


def _sel_dot_r(x, sel):
    return jnp.dot(jnp.concatenate(_split3(x), axis=1), jnp.concatenate([sel] * 3, axis=0),
                   preferred_element_type=F32)


def _scan_masks(q, fwd):
    ri = lax.broadcasted_iota(jnp.int32, (q, q), 0)
    ci = lax.broadcasted_iota(jnp.int32, (q, q), 1)
    ahead = (ri - ci) * jnp.where(fwd, 1, -1)
    return ahead >= 0, ahead > 0


def _mod_kernel(c_ref, w_ref, b_ref, o_ref):
    s = _silu(c_ref[...])
    o_ref[0, 0] = _bdot(s, w_ref[0]) + b_ref[0, 0]


def _mod_table(cvec, ada_w, ada_b):
    depth = ada_w.shape[0]
    out = pl.pallas_call(
        _mod_kernel,
        grid=(depth, N_MOD),
        in_specs=[_const_spec((SUBLANES, D)),
                  pl.BlockSpec((1, D, D), lambda l, j: (l, 0, j)),
                  pl.BlockSpec((1, 1, 1, D), lambda l, j: (l, j, 0, 0))],
        out_specs=pl.BlockSpec((1, 1, SUBLANES, D), lambda l, j: (l, j, 0, 0)),
        out_shape=jax.ShapeDtypeStruct((depth, N_MOD, SUBLANES, D), F32),
        compiler_params=_cparams(2),
        name="mod_table",
    )(cvec, ada_w, ada_b.reshape(depth, N_MOD, 1, D))
    return jnp.transpose(out, (0, 2, 1, 3))


def _ffn_kernel(h_ref, mod_ref, nw_ref, wup_ref, wdn_ref, *rest, base, final):
    o_ref = rest[-1]
    x = h_ref[0]
    mod = mod_ref[0]
    xm = _rms(x, nw_ref[...]) * (1.0 + mod[base + 1:base + 2]) + mod[base:base + 1]
    xb = xm.astype(BF16)
    g = jnp.dot(xb, wup_ref[:, :D_FF], preferred_element_type=F32)
    u = jnp.dot(xb, wup_ref[:, D_FF:], preferred_element_type=F32)
    y = jnp.dot((_silu(g) * u).astype(BF16), wdn_ref[...], preferred_element_type=F32)
    out = x + (0.5 * mod[base + 2:base + 3]) * y
    if final:
        out = _rms(out, rest[0][...])
    o_ref[0] = out


def _ffn(h, mod, mod_row, nw, w_up, w_dn, base, final_w=None):
    bsz, n, _ = h.shape
    row = (lambda b: b) if mod_row is None else (lambda b: mod_row)
    in_specs = [pl.BlockSpec((1, TM, D), lambda b, t: (b, t, 0)),
                pl.BlockSpec((1, N_MOD, D), lambda b, t: (row(b), 0, 0)),
                _const_spec((1, D)),
                _const_spec((D, 2 * D_FF)),
                _const_spec((D_FF, D))]
    args = [h, mod, nw.reshape(1, D), w_up, w_dn]
    if final_w is not None:
        in_specs.append(_const_spec((1, D)))
        args.append(final_w.reshape(1, D))
    return pl.pallas_call(
        functools.partial(_ffn_kernel, base=base, final=final_w is not None),
        grid=(bsz, n // TM),
        in_specs=in_specs,
        out_specs=pl.BlockSpec((1, TM, D), lambda b, t: (b, t, 0)),
        out_shape=jax.ShapeDtypeStruct(h.shape, F32),
        compiler_params=_cparams(2),
        name="ffn",
    )(*args)


def _lat_tile_spec(n, col_rows):
    if col_rows is None:
        return (1, TM, D), (lambda a: a)
    wpt = TM // col_rows
    return (1, col_rows, wpt * D), (lambda a: a.reshape(a.shape[0], col_rows, GRID_W * D))


def _tile_rows(blk, col_rows):
    if col_rows is None:
        return blk
    wpt = TM // col_rows
    return jnp.concatenate([blk[:, k * D:(k + 1) * D] for k in range(wpt)], axis=0)


def _tile_unrows(x, col_rows):
    if col_rows is None:
        return x
    wpt = TM // col_rows
    return jnp.concatenate([x[k * col_rows:(k + 1) * col_rows] for k in range(wpt)], axis=1)


def _inproj_kernel(mod_ref, hl_ref, hp_ref, hn_ref, hc_ref, nw_ref, wc_ref, wr_ref, cw_ref, cb_ref,
                   oc_ref, or_ref, ou_ref, *, nt, col_rows):
    t = pl.program_id(1)
    is_ctx = t == nt
    mod = mod_ref[0]

    def modulated(x):
        return _rms(x, nw_ref[...]) * (1.0 + mod[4:5]) + mod[3:4]

    has_prev = jnp.logical_and(jnp.logical_not(is_ctx), t >= 1)
    has_next = jnp.logical_and(jnp.logical_not(is_ctx), t <= nt - 2)
    x_prev = modulated(hp_ref[0])
    x_next = modulated(hn_ref[0])
    x_prev = jnp.where(has_prev, x_prev, jnp.zeros_like(x_prev))
    x_next = jnp.where(has_next, x_next, jnp.zeros_like(x_next))
    x_main = modulated(jnp.where(is_ctx, hc_ref[0], _tile_rows(hl_ref[0], col_rows)))
    xb = jnp.concatenate([x_prev, x_main, x_next], axis=0).astype(BF16)
    pc = jnp.dot(xb, wc_ref[...], preferred_element_type=F32)
    acc = cb_ref[...]
    for k in range(CONV_K):
        off = HALO - CONV_K // 2 + k
        acc = acc + cw_ref[k:k + 1, :] * pc[off:off + TM]
    oc_ref[0] = _silu(acc)
    pr = jnp.dot(x_main.astype(BF16), wr_ref[...], preferred_element_type=F32)
    or_ref[0] = pr[:, :REST_W]
    for k in range(S5_SPLIT):
        ou_ref[0, k] = pr[:, REST_W + k * LANES:REST_W + (k + 1) * LANES]


def _inproj(h_lat, h_ctx, mod, nw, wc, wr, cw, cb, col_rows):
    bsz, n, _ = h_lat.shape
    nt = n // TM
    s_tot = n + TM
    tile_shape, view = _lat_tile_spec(n, col_rows)
    lat_t = lambda t: jnp.minimum(t, nt - 1)
    if col_rows is None:
        hb = TM // HALO
        nb = n // HALO
        tile_map = lambda b, t: (b, lat_t(t), 0)
        prev_map = lambda b, t: (b, jnp.clip(t * hb - 1, 0, nb - 1), 0)
        next_map = lambda b, t: (b, jnp.clip((t + 1) * hb, 0, nb - 1), 0)
    else:
        wpt = TM // col_rows
        tile_map = lambda b, t: (b, 0, lat_t(t))
        prev_map = lambda b, t: (b, col_rows // HALO - 1, jnp.clip(t * wpt - 1, 0, GRID_W - 1))
        next_map = lambda b, t: (b, 0, jnp.clip((t + 1) * wpt, 0, GRID_W - 1))
    hv = view(h_lat)
    return pl.pallas_call(
        functools.partial(_inproj_kernel, nt=nt, col_rows=col_rows),
        grid=(bsz, nt + 1),
        in_specs=[pl.BlockSpec((1, N_MOD, D), lambda b, t: (jnp.where(t == nt, bsz, b), 0, 0)),
                  pl.BlockSpec(tile_shape, tile_map),
                  pl.BlockSpec((1, HALO, D), prev_map),
                  pl.BlockSpec((1, HALO, D), next_map),
                  pl.BlockSpec((1, TM, D), lambda b, t: (b, 0, 0)),
                  _const_spec((1, D)),
                  _const_spec((D, CONV_W)),
                  _const_spec((D, REST_W + BW)),
                  _const_spec((SUBLANES, CONV_W)),
                  _const_spec((1, CONV_W))],
        out_specs=[pl.BlockSpec((1, TM, CONV_W), lambda b, t: (b, t, 0)),
                   pl.BlockSpec((1, TM, REST_W), lambda b, t: (b, t, 0)),
                   pl.BlockSpec((1, S5_SPLIT, TM, LANES), lambda b, t: (b, 0, t, 0))],
        out_shape=[jax.ShapeDtypeStruct((bsz, s_tot, CONV_W), F32),
                   jax.ShapeDtypeStruct((bsz, s_tot, REST_W), F32),
                   jax.ShapeDtypeStruct((bsz, S5_SPLIT, s_tot, LANES), F32)],
        compiler_params=_cparams(2),
        name="inproj",
    )(mod, hv, hv, hv, h_ctx, nw.reshape(1, D), wc, wr, cw, cb)


def _chunk_index(n_lat, q):
    ncl, ncc = n_lat // q, TM // q

    def mem_chunk(d, j):
        ctx = jnp.where(d == 0, ncl + j, ncl + ncc - 1 - j)
        lat = jnp.where(d == 0, j - ncc, ncl - 1 - (j - ncc))
        return jnp.where(j < ncc, ctx, lat)
    return mem_chunk, ncl + ncc


def _ssd_kernel(xbc_ref, sm_ref, par_ref, parx_ref, ex_ref, y_ref, h_ref):
    d = pl.program_id(1)

    @pl.when(pl.program_id(2) == 0)
    def _():
        h_ref[...] = jnp.zeros_like(h_ref)

    q = SSM_Q
    fwd = d == 0
    xbc = xbc_ref[0]
    xs = xbc[:, :BW]
    bm = xbc[:, BW:BW + SSM_G * SSM_N].astype(BF16)
    cm = xbc[:, BW + SSM_G * SSM_N:].astype(BF16)
    par = par_ref[0]
    lane = lax.broadcasted_iota(jnp.int32, (1, LANES), 1)
    dt = jnp.where(lane < SSM_H, _softplus(sm_ref[0] + par[0:1]), 0.0)
    da = dt * (-jnp.exp(par[1:2]))
    incl, _ = _scan_masks(q, fwd)
    sel = _scan_sel3(q, fwd)
    g = _sel_dot_l(sel, da)
    g_t = g.T
    dtx = _sel_dot_r(dt, ex_ref[...])
    dax = dtx * (-jnp.exp(parx_ref[0, 0:1]))
    gx = _sel_dot_l(sel, dax)
    glx = jnp.sum(dax, axis=0, keepdims=True)
    xdt = xs * dtx
    xdt_b = xdt.astype(BF16)
    half = (lax.broadcasted_iota(jnp.int32, (1, BW), 1) % (2 * SSM_P)) < SSM_P
    x_even = jnp.where(half, xdt_b, jnp.zeros_like(xdt_b))
    x_odd = jnp.where(half, jnp.zeros_like(xdt_b), xdt_b)
    hpg = SSM_H // SSM_G
    cb = [_bdot_nt(cm[:, k * SSM_N:(k + 1) * SSM_N], bm[:, k * SSM_N:(k + 1) * SSM_N])
          for k in range(SSM_G)]

    def scores(h):
        dec = jnp.exp(jnp.where(incl, g[:, h:h + 1] - g_t[h:h + 1, :], -jnp.inf))
        return (cb[h // hpg] * dec).astype(BF16)

    pairs = []
    for p in range(SSM_H // 2):
        cols = slice(p * 2 * SSM_P, (p + 1) * 2 * SSM_P)
        pairs.append(jnp.dot(scores(2 * p), x_even[:, cols], preferred_element_type=F32)
                     + jnp.dot(scores(2 * p + 1), x_odd[:, cols], preferred_element_type=F32))
    y = jnp.concatenate(pairs, axis=1)
    h_prev = h_ref[...]
    hb = h_prev.astype(BF16)
    gw = hpg * SSM_P
    y_inter = jnp.concatenate(
        [jnp.dot(cm[:, k * SSM_N:(k + 1) * SSM_N], hb[:, k * gw:(k + 1) * gw], preferred_element_type=F32)
         for k in range(SSM_G)], axis=1)
    y = y + y_inter * jnp.exp(gx)
    xw = (xdt * jnp.exp(glx - gx)).astype(BF16)
    st = jnp.concatenate(
        [_bdot_tn(bm[:, k * SSM_N:(k + 1) * SSM_N], xw[:, k * gw:(k + 1) * gw]) for k in range(SSM_G)],
        axis=1)
    h_ref[...] = jnp.exp(glx) * h_prev + st
    skip = jnp.where(fwd, parx_ref[0, 1:2], 0.0)
    y_ref[0, 0] = y + skip * xs


def _ssd(conv_out, rest_out, mixpar, ssm_parx, expand, n_lat):
    bsz, s_tot, _ = conv_out.shape
    mem_chunk, nc = _chunk_index(n_lat, SSM_Q)
    sm_blk = (REST_W - N_SMALL) // LANES
    return pl.pallas_call(
        _ssd_kernel,
        grid=(bsz, 2, nc),
        in_specs=[pl.BlockSpec((1, SSM_Q, 2 * BW), lambda b, d, j: (b, mem_chunk(d, j), 0)),
                  pl.BlockSpec((1, SSM_Q, LANES), lambda b, d, j: (b, mem_chunk(d, j), sm_blk + d)),
                  pl.BlockSpec((1, SUBLANES, LANES), lambda b, d, j: (d, 0, 0)),
                  pl.BlockSpec((1, SUBLANES, BW), lambda b, d, j: (d, 0, 0)),
                  _const_spec((LANES, BW))],
        out_specs=pl.BlockSpec((1, 1, SSM_Q, BW), lambda b, d, j: (d, b, mem_chunk(d, j), 0)),
        out_shape=jax.ShapeDtypeStruct((2, bsz, s_tot, BW), F32),
        scratch_shapes=[pltpu.VMEM((SSM_N, BW), F32)],
        compiler_params=_cparams(3),
        name="ssd",
    )(conv_out, rest_out, mixpar, ssm_parx, expand)


def _mlstm_kernel(qk_ref, v_ref, sm_ref, par_ref, y_ref, c_ref, m_ref):
    d = pl.program_id(1)

    @pl.when(pl.program_id(2) == 0)
    def _():
        c_ref[...] = jnp.zeros_like(c_ref)
        m_ref[...] = jnp.zeros_like(m_ref)

    q = ML_Q
    fwd = d == 0
    qk = qk_ref[0]
    v = v_ref[0]
    pre = sm_ref[0] + par_ref[0, 0:1]
    lf = -_softplus(-pre)
    incl, _ = _scan_masks(q, fwd)
    bt = _sel_dot_l(_scan_sel3(q, fwd), lf)
    pre_t = pre.T
    bt_t = bt.T
    tot = jnp.sum(lf, axis=0, keepdims=True)
    ones = jnp.ones((q, ML_DV), F32)
    outs = []
    for h in range(ML_H):
        qh = qk[:, h * ML_DK:(h + 1) * ML_DK] * (ML_DK ** -0.5)
        kh = qk[:, ML_H * ML_DK + h * ML_DK:ML_H * ML_DK + (h + 1) * ML_DK]
        vext = jnp.concatenate([v[:, h * ML_DV:(h + 1) * ML_DV], ones], axis=1).astype(BF16)
        ig_c, ig_r = pre[:, L_IG + h:L_IG + h + 1], pre_t[L_IG + h:L_IG + h + 1, :]
        bt_c, bt_r = bt[:, L_FG + h:L_FG + h + 1], bt_t[L_FG + h:L_FG + h + 1, :]
        bl = tot[:, L_FG + h:L_FG + h + 1]
        dmat = jnp.where(incl, bt_c - bt_r, -jnp.inf) + ig_r
        m_intra = jnp.max(dmat, axis=1, keepdims=True)
        m_loc = jnp.max(bl - bt_r + ig_r, axis=1, keepdims=True)
        w_end = jnp.exp(bl - bt_c + ig_c - m_loc)
        c_loc = _bdot_tn(kh * w_end, vext)
        c_prev = c_ref[h]
        m_prev = m_ref[h:h + 1, 0:1]
        e = bt_c + m_prev
        m_out = jnp.maximum(e, m_intra)
        w_intra = jnp.exp(dmat - m_out) * _bdot_nt(qh, kh)
        w_x = jnp.exp(e - m_out)
        qc = _bdot(qh, c_prev)
        num = _bdot(w_intra, vext[:, :ML_DV]) + w_x * qc[:, :ML_DV]
        den = jnp.sum(w_intra, axis=1, keepdims=True) + w_x * qc[:, ML_DV:ML_DV + 1]
        den = jnp.maximum(jnp.abs(den), jnp.exp(-m_out))
        outs.append(num / den)
        m_new = jnp.maximum(bl + m_prev, m_loc)
        c_ref[h] = jnp.exp(bl + m_prev - m_new) * c_prev + jnp.exp(m_loc - m_new) * c_loc
        m_ref[h:h + 1, :] = jnp.broadcast_to(m_new, (1, LANES))
    y_ref[0, 0] = jnp.concatenate(outs, axis=1)


def _mlstm(conv_out, rest_out, mixpar, n_lat):
    bsz, s_tot, _ = conv_out.shape
    mem_chunk, nc = _chunk_index(n_lat, ML_Q)
    sm_blk = (REST_W - N_SMALL) // LANES
    return pl.pallas_call(
        _mlstm_kernel,
        grid=(bsz, 2, nc),
        in_specs=[pl.BlockSpec((1, ML_Q, BW), lambda b, d, j: (b, mem_chunk(d, j), 2)),
                  pl.BlockSpec((1, ML_Q, BW), lambda b, d, j: (b, mem_chunk(d, j), 1)),
                  pl.BlockSpec((1, ML_Q, LANES), lambda b, d, j: (b, mem_chunk(d, j), sm_blk + d)),
                  pl.BlockSpec((1, SUBLANES, LANES), lambda b, d, j: (d, 0, 0))],
        out_specs=pl.BlockSpec((1, 1, ML_Q, BW), lambda b, d, j: (d, b, mem_chunk(d, j), 0)),
        out_shape=jax.ShapeDtypeStruct((2, bsz, s_tot, BW), F32),
        scratch_shapes=[pltpu.VMEM((ML_H, ML_DK, 2 * ML_DV), F32),
                        pltpu.VMEM((SUBLANES, LANES), F32)],
        compiler_params=_cparams(3),
        name="mlstm",
    )(conv_out, rest_out, rest_out, mixpar)


def _block_diag(x, n_blocks):
    r, cols = x.shape
    w = cols // n_blocks
    tiled = jnp.concatenate([x] * n_blocks, axis=0)
    assert r & (r - 1) == 0 and w & (w - 1) == 0
    ri = lax.shift_right_logical(lax.broadcasted_iota(jnp.int32, tiled.shape, 0), r.bit_length() - 1)
    ci = lax.shift_right_logical(lax.broadcasted_iota(jnp.int32, tiled.shape, 1), w.bit_length() - 1)
    return jnp.where(ri == ci, tiled, jnp.zeros_like(tiled))


def _split_hi_lo(x):
    hi = x.astype(BF16)
    return hi, (x - hi.astype(F32)).astype(BF16)


def _dot_hi_blocks(lhs, rhs_cat, n_blocks):
    lh, ll = _split_hi_lo(lhs)
    rh, rl = _split_hi_lo(rhs_cat)
    rh, rl = _block_diag(rh, n_blocks), _block_diag(rl, n_blocks)
    return jnp.dot(jnp.concatenate([lh, lh, ll], axis=1), jnp.concatenate([rh, rl, rh], axis=0),
                   preferred_element_type=F32)


def _unit_lower_inverse_blocks(l_cat, n_blocks):
    q = l_cat.shape[0]
    ri = lax.broadcasted_iota(jnp.int32, l_cat.shape, 0)
    ci = lax.broadcasted_iota(jnp.int32, l_cat.shape, 1) & (q - 1)
    p = -l_cat
    t = jnp.where(ri == ci, 1.0, 0.0) + p
    p = _dot_hi_blocks(p, p, n_blocks)
    levels = max(q.bit_length() - 2, 0)
    for k in range(levels):
        if k == levels - 1:
            return t + _dot_hi_blocks(t, p, n_blocks)
        both = _dot_hi_blocks(jnp.concatenate([p, t], axis=0), p, n_blocks)
        p, t = both[:q], t + both[q:]
    return t


def _head_l2norm(x, width, scale):
    parts = []
    for h in range(x.shape[1] // width):
        seg = x[:, h * width:(h + 1) * width]
        parts.append(seg * (lax.rsqrt(jnp.sum(seg * seg, axis=-1, keepdims=True) + EPS) * scale))
    return jnp.concatenate(parts, axis=1)


def _gdn_kernel(qkv_ref, sm_ref, par_ref, ex_ref, y_ref, s_ref):
    d = pl.program_id(1)

    @pl.when(pl.program_id(2) == 0)
    def _():
        s_ref[...] = jnp.zeros_like(s_ref)

    q, nh = GDN_Q, GDN_H
    hk = nh * GDN_DK
    fwd = d == 0
    qkv = qkv_ref[0]
    qn = _head_l2norm(qkv[:, :hk], GDN_DK, GDN_DK ** -0.5)
    kn = _head_l2norm(qkv[:, hk:2 * hk], GDN_DK, 1.0)
    v = qkv[:, 2 * hk:]
    sm = sm_ref[0]
    par = par_ref[0]
    lane = lax.broadcasted_iota(jnp.int32, (1, LANES), 1)
    lg = -jnp.exp(par[1:2]) * _softplus(sm + par[0:1])
    g = _sel_dot_l(_scan_sel3(q, fwd), lg)
    packed = jnp.where(lane < L_GB, g, _sigmoid(sm))
    wide = _sel_dot_r(packed, ex_ref[...])
    g_c = wide[:, :nh * q]
    gx = wide[:, nh * q:nh * q + hk]
    bx = wide[:, nh * q + hk:]
    glx = jnp.where(fwd, gx[q - 1:q], gx[0:1])
    ri = lax.broadcasted_iota(jnp.int32, (q, nh * q), 0)
    ci = lax.broadcasted_iota(jnp.int32, (q, nh * q), 1) & (q - 1)
    diag = jnp.where(ri == ci, g_c, 0.0)
    g_r = _sel_dot_l(_scan_sel3(q, fwd, all_ones=True), diag)
    ahead = (ri - ci) * jnp.where(fwd, 1, -1)
    diff = g_c - g_r
    kb = kn * bx
    k_bd = _block_diag(kn.astype(BF16), nh)
    la = _bdot_nt(jnp.concatenate([kb, qn], axis=0), k_bd)
    l_cat = la[:q] * jnp.exp(jnp.where(ahead > 0, diff, -jnp.inf))
    a_qk = la[q:] * jnp.exp(jnp.where(ahead >= 0, diff, -jnp.inf))
    t_inv = _unit_lower_inverse_blocks(l_cat, nh)
    eg = jnp.exp(gx)
    rhs = jnp.concatenate([_block_diag((v * bx).astype(BF16), nh), _block_diag((kb * eg).astype(BF16), nh)], axis=1)
    uw = jnp.dot(t_inv.astype(BF16), rhs, preferred_element_type=F32)
    s_prev = s_ref[...]
    ws = jnp.dot(jnp.concatenate([uw[:, hk:], qn * eg], axis=0).astype(BF16),
                 _block_diag(s_prev.astype(BF16), nh), preferred_element_type=F32)
    u_bd = _block_diag((uw[:, :hk] - ws[:q]).astype(BF16), nh)
    y_ref[0, 0] = ws[q:] + jnp.dot(a_qk.astype(BF16), u_bd, preferred_element_type=F32)
    kd = (kn * jnp.exp(glx - gx)).astype(BF16)
    kd_rows = jnp.concatenate([kd[:, h * GDN_DK:(h + 1) * GDN_DK] for h in range(nh)], axis=0)
    s_ref[...] = jnp.exp(glx) * s_prev + _bdot_tn(kd_rows, u_bd)


def _gdn_expand():
    src = jnp.arange(LANES)[:, None]
    over_q = src == L_GA + jnp.arange(GDN_H * GDN_Q)[None, :] // GDN_Q
    over_dk = jnp.arange(GDN_H * GDN_DK)[None, :] // GDN_DK
    return jnp.concatenate([over_q, src == L_GA + over_dk, src == L_GB + over_dk], axis=1).astype(BF16)


def _gdn(conv_out, rest_out, mixpar, expand, n_lat):
    bsz, s_tot, _ = conv_out.shape
    mem_chunk, nc = _chunk_index(n_lat, GDN_Q)
    sm_blk = (REST_W - N_SMALL) // LANES
    return pl.pallas_call(
        _gdn_kernel,
        grid=(bsz, 2, nc),
        in_specs=[pl.BlockSpec((1, GDN_Q, 3 * BW), lambda b, d, j: (b, mem_chunk(d, j), 1)),
                  pl.BlockSpec((1, GDN_Q, LANES), lambda b, d, j: (b, mem_chunk(d, j), sm_blk + d)),
                  pl.BlockSpec((1, SUBLANES, LANES), lambda b, d, j: (d, 0, 0)),
                  _const_spec(expand.shape)],
        out_specs=pl.BlockSpec((1, 1, GDN_Q, BW), lambda b, d, j: (d, b, mem_chunk(d, j), 0)),
        out_shape=jax.ShapeDtypeStruct((2, bsz, s_tot, BW), F32),
        scratch_shapes=[pltpu.VMEM((GDN_DK, GDN_H * GDN_DV), F32)],
        compiler_params=_cparams(3),
        name="gdn",
    )(conv_out, rest_out, mixpar, expand)


def _s5_scan(sloc, h_init, pw, fwd, n_valid):
    n = sloc.shape[0]
    row = lax.broadcasted_iota(jnp.int32, (n, 1), 0)

    def mul(x, i):
        return pw[2 * i:2 * i + 1] * x + pw[2 * i + 1:2 * i + 2] * pltpu.roll(x, S5_P, axis=1)

    pos = jnp.where(fwd, row, n - 1 - row)
    first = jnp.where(fwd, 0, n_valid - 1)
    z = pltpu.roll(sloc, jnp.where(fwd, 1, n - 1), axis=0)
    z = jnp.where(pos >= jnp.where(fwd, 1, n - n_valid + 1), z, 0.0)
    z = jnp.where(row == first, h_init, z)
    s, i = 1, 0
    while s < n:
        zs = pltpu.roll(z, jnp.where(fwd, s, n - s), axis=0)
        zs = jnp.where(pos >= s, zs, 0.0)
        z = z + mul(zs, i)
        s, i = 2 * s, i + 1
    last = jnp.where(fwd, n_valid - 1, 0)
    pick = row == last
    z_last = jnp.sum(jnp.where(pick, z, 0.0), axis=0, keepdims=True)
    s_last = jnp.sum(jnp.where(pick, sloc, 0.0), axis=0, keepdims=True)
    return z, mul(z_last, 0) + s_last


def _s5_segment(load_rows, store_rows, n_rows, n_valid, fwd, ut_ref, st_ref, wcat_ref, pm_ref, pw_ref,
                skip_row):
    qg = S5_Q * S5_GC
    for t in range(S5_Q):
        vt = load_rows(t)
        ut_ref[:, t * S5_GC:(t + 1) * S5_GC, :] = vt.T.reshape(S5_G, S5_GC, n_rows)

    def group(gi, carry):
        ug = ut_ref[gi].T.astype(BF16)
        y1 = jnp.dot(ug, wcat_ref[0, gi], preferred_element_type=F32)
        h_start, h_end = _s5_scan(y1[:, qg:], st_ref[gi, 0:1, :], pw_ref[0, gi], fwd, n_valid)
        st_ref[gi, 0:1, :] = h_end
        yy = y1[:, :qg] + jnp.dot(h_start.astype(BF16), pm_ref[0, gi], preferred_element_type=F32)
        ut_ref[gi] = yy.T
        return carry

    lax.fori_loop(0, S5_G, group, 0)
    for t in range(S5_Q):
        wt = ut_ref[:, t * S5_GC:(t + 1) * S5_GC, :].reshape(BW, n_rows).T
        store_rows(t, wt + skip_row * load_rows(t))


def _load_token(u_ref, t, n_chunks):
    return jnp.concatenate([u_ref[0, k, pl.ds(t, n_chunks, stride=S5_Q), :] for k in range(S5_SPLIT)], axis=1)


def _store_token(y_ref, t, n_chunks, val):
    for k in range(S5_SPLIT):
        y_ref[0, 0, k, pl.ds(t, n_chunks, stride=S5_Q), :] = val[:, k * LANES:(k + 1) * LANES]


def _s5_kernel(ul_ref, uc_ref, wcat_ref, pm_ref, pw_ref, dsk_ref, yl_ref, yc_ref, ut_ref, utc_ref, st_ref,
               *, n_seg_chunks):
    d = pl.program_id(1)
    fwd = d == 0
    skip_row = jnp.where(fwd, dsk_ref[...], 0.0)
    ncc = TM // S5_Q

    @pl.when(pl.program_id(2) == 0)
    def _():
        st_ref[...] = jnp.zeros_like(st_ref)
        pad = jnp.zeros((LANES - ncc, BW), F32)

        def load_c(t):
            return jnp.concatenate([_load_token(uc_ref, t, ncc), pad], axis=0)

        def store_c(t, val):
            _store_token(yc_ref, t, ncc, val[:ncc])

        _s5_segment(load_c, store_c, LANES, ncc, fwd, utc_ref, st_ref, wcat_ref, pm_ref, pw_ref, skip_row)

    def load_l(t):
        return _load_token(ul_ref, t, n_seg_chunks)

    def store_l(t, val):
        _store_token(yl_ref, t, n_seg_chunks, val)

    _s5_segment(load_l, store_l, n_seg_chunks, n_seg_chunks, fwd, ut_ref, st_ref, wcat_ref, pm_ref, pw_ref,
                skip_row)


def _s5(u_slabs, wcat, pm, pw, dskip, n_lat):
    bsz = u_slabs.shape[0]
    seg = min(S5_SEG, n_lat)
    nseg = n_lat // seg
    nsc = seg // S5_Q
    qg = S5_Q * S5_GC
    seg_mem = lambda d, s: jnp.where(d == 0, s, nseg - 1 - s)
    ctx_blk = n_lat // TM
    yl, yc = pl.pallas_call(
        functools.partial(_s5_kernel, n_seg_chunks=nsc),
        grid=(bsz, 2, nseg),
        in_specs=[pl.BlockSpec((1, S5_SPLIT, seg, LANES), lambda b, d, s: (b, 0, seg_mem(d, s), 0)),
                  pl.BlockSpec((1, S5_SPLIT, TM, LANES), lambda b, d, s: (b, 0, ctx_blk, 0)),
                  pl.BlockSpec((1, S5_G, qg, qg + 2 * S5_P), lambda b, d, s: (d, 0, 0, 0)),
                  pl.BlockSpec((1, S5_G, 2 * S5_P, qg), lambda b, d, s: (d, 0, 0, 0)),
                  pl.BlockSpec((1, S5_G, 2 * SUBLANES, LANES), lambda b, d, s: (d, 0, 0, 0)),
                  _const_spec((1, BW))],
        out_specs=[pl.BlockSpec((1, 1, S5_SPLIT, seg, LANES), lambda b, d, s: (d, b, 0, seg_mem(d, s), 0)),
                   pl.BlockSpec((1, 1, S5_SPLIT, TM, LANES), lambda b, d, s: (d, b, 0, 0, 0))],
        out_shape=[jax.ShapeDtypeStruct((2, bsz, S5_SPLIT, n_lat, LANES), F32),
                   jax.ShapeDtypeStruct((2, bsz, S5_SPLIT, TM, LANES), F32)],
        scratch_shapes=[pltpu.VMEM((S5_G, qg, nsc), F32),
                        pltpu.VMEM((S5_G, qg, LANES), F32),
                        pltpu.VMEM((S5_G, SUBLANES, LANES), F32)],
        compiler_params=_cparams(3),
        name="s5",
    )(u_slabs, u_slabs, wcat, pm, pw, dskip)
    return yl, yc


def _s5_operators(a_re, a_im, log_dt, b_re, b_im, c_re, c_im):
    hi = lax.Precision.HIGHEST
    q, nq = S5_Q, S5_Q + 1
    lre = jnp.minimum(a_re.astype(F32), -1e-4)
    lim = a_im.astype(F32)
    dt = jnp.exp(log_dt.astype(F32))[..., None]
    mag = jnp.exp(lre * dt)
    ab_re, ab_im = mag * jnp.cos(lim * dt), mag * jnp.sin(lim * dt)
    den = lre * lre + lim * lim
    f_re = ((ab_re - 1.0) * lre + ab_im * lim) / den
    f_im = (ab_im * lre - (ab_re - 1.0) * lim) / den
    br, bi = b_re.astype(F32), b_im.astype(F32)
    bb_re = f_re[..., None] * br - f_im[..., None] * bi
    bb_im = f_re[..., None] * bi + f_im[..., None] * br
    pr, pi = [jnp.ones_like(ab_re)], [jnp.zeros_like(ab_im)]
    for _ in range(q):
        pr, pi = pr + [pr[-1] * ab_re - pi[-1] * ab_im], pi + [pr[-1] * ab_im + pi[-1] * ab_re]
    pw_re, pw_im = jnp.stack(pr, axis=2), jnp.stack(pi, axis=2)
    e_re = pw_re[..., None] * bb_re[:, :, None] - pw_im[..., None] * bb_im[:, :, None]
    e_im = pw_re[..., None] * bb_im[:, :, None] + pw_im[..., None] * bb_re[:, :, None]
    cr, ci = c_re.astype(F32), c_im.astype(F32)
    kern = (jnp.einsum('gcp,dgnpe->dgnce', cr, e_re, precision=hi)
            - jnp.einsum('gcp,dgnpe->dgnce', ci, e_im, precision=hi))
    s_idx = jnp.arange(q)[:, None]
    t_idx = jnp.arange(q)[None, :]
    mats, rmats, pmats = [], [], []
    for d in range(2):
        lag = (t_idx - s_idx) if d == 0 else (s_idx - t_idx)
        valid = lag >= 0
        m = kern[d][:, jnp.clip(lag, 0, q)]
        m = jnp.where(valid[None, :, :, None, None], m, 0.0)
        mats.append(jnp.transpose(m, (0, 1, 4, 2, 3)).reshape(S5_G, q * S5_GC, q * S5_GC))
        e_s = (q - 1 - jnp.arange(q)) if d == 0 else jnp.arange(q)
        r = jnp.concatenate([e_re[d][:, e_s], e_im[d][:, e_s]], axis=2)
        rmats.append(jnp.transpose(r, (0, 1, 3, 2)).reshape(S5_G, q * S5_GC, 2 * S5_P))
        f_t = (jnp.arange(q) + 1) if d == 0 else (q - jnp.arange(q))
        fr, fi = pw_re[d][:, f_t], pw_im[d][:, f_t]
        p_from_re = cr[:, None] * fr[:, :, None] - ci[:, None] * fi[:, :, None]
        p_from_im = -cr[:, None] * fi[:, :, None] - ci[:, None] * fr[:, :, None]
        p = jnp.concatenate([p_from_re, p_from_im], axis=3)
        pmats.append(jnp.transpose(p, (0, 3, 1, 2)).reshape(S5_G, 2 * S5_P, q * S5_GC))
    wcat = jnp.concatenate([jnp.stack(mats), jnp.stack(rmats)], axis=3).astype(BF16)
    pm = jnp.stack(pmats).astype(BF16)
    dr, di = pw_re[:, :, q], pw_im[:, :, q]
    rows = []
    for _ in range(SUBLANES):
        rows += [jnp.concatenate([dr, dr], axis=-1), jnp.concatenate([-di, di], axis=-1)]
        dr, di = dr * dr - di * di, 2.0 * dr * di
    return wcat, pm, jnp.stack(rows, axis=2)


def _head_rms(x, w, width):
    parts = []
    for h in range(x.shape[1] // width):
        seg = x[:, h * width:(h + 1) * width]
        parts.append(seg * lax.rsqrt(jnp.mean(seg * seg, axis=-1, keepdims=True) + EPS))
    return jnp.concatenate(parts, axis=1) * w


def _merge_kernel(h_ref, mod_ref, nw_ref, wg_ref, gb_ref, ya_ref, yb_ref, yc_ref, yd_ref, za_ref, ob_ref,
                  zd_ref, na_ref, nb_ref, nd_ref, glw_ref, glb_ref, wbr_ref, wo_ref, o_ref, *, col_rows):
    x = _tile_rows(h_ref[0], col_rows)
    mod = mod_ref[0]
    xb = (_rms(x, nw_ref[...]) * (1.0 + mod[4:5]) + mod[3:4]).astype(BF16)
    gates = _sigmoid(jnp.dot(xb, wg_ref[...], preferred_element_type=F32) + gb_ref[...])
    ya = _rms((ya_ref[0, 0] + ya_ref[1, 0]) * _silu(za_ref[0]), na_ref[...])
    yb = _sigmoid(ob_ref[0]) * _head_rms(yb_ref[0, 0] + yb_ref[1, 0], nb_ref[...], ML_DV)
    gc = _gelu_tanh(jnp.concatenate([yc_ref[0, 0, k] + yc_ref[1, 0, k] for k in range(S5_SPLIT)], axis=1))
    yc = gc * _sigmoid(_bdot(gc, glw_ref[...]) + glb_ref[...])
    yd = _head_rms(yd_ref[0, 0] + yd_ref[1, 0], nd_ref[...], GDN_DV) * _silu(zd_ref[0])
    merged = jnp.zeros((TM, D), F32)
    for k, yk in enumerate((ya, yb, yc, yd)):
        merged = merged + gates[:, k * D:(k + 1) * D] * _bdot(yk, wbr_ref[k])
    out = _bdot(merged, wo_ref[...])
    o_ref[0] = _tile_unrows(x + mod[5:6] * out, col_rows)


def _merge(h, mod, mod_row, tile0, nw, wg, gb, ys, rest_out, norms, glw, glb, wbr, wo, col_rows):
    bsz, n, _ = h.shape
    nt = n // TM
    tile_shape, view = _lat_tile_spec(n, col_rows)
    tile_map = (lambda b, t: (b, t, 0)) if col_rows is None else (lambda b, t: (b, 0, t))
    row = (lambda b: b) if mod_row is None else (lambda b: mod_row)
    y_spec = lambda t0: pl.BlockSpec((2, 1, TM, BW), lambda b, t: (0, b, t0 + t, 0))
    s5_spec = lambda t0: pl.BlockSpec((2, 1, S5_SPLIT, TM, LANES), lambda b, t: (0, b, 0, t0 + t, 0))
    specs = (y_spec, y_spec, s5_spec, y_spec)
    r_spec = lambda c: pl.BlockSpec((1, TM, BW), lambda b, t: (b, tile0 + t, c))
    hv = view(h)
    out = pl.pallas_call(
        functools.partial(_merge_kernel, col_rows=col_rows),
        grid=(bsz, nt),
        in_specs=[pl.BlockSpec(tile_shape, tile_map),
                  pl.BlockSpec((1, N_MOD, D), lambda b, t: (row(b), 0, 0)),
                  _const_spec((1, D)),
                  _const_spec((D, N_BRANCH * D)),
                  _const_spec((1, N_BRANCH * D)),
                  *[mk(t0) for mk, (_, t0) in zip(specs, ys)],
                  r_spec(REST_COL['ssm_z']), r_spec(REST_COL['ml_o']), r_spec(REST_COL['gdn_z']),
                  _const_spec((1, BW)), _const_spec((1, BW)), _const_spec((1, BW)),
                  _const_spec((BW, BW)), _const_spec((1, BW)),
                  _const_spec((N_BRANCH, BW, D)),
                  _const_spec((D, D))],
        out_specs=pl.BlockSpec(tile_shape, tile_map),
        out_shape=jax.ShapeDtypeStruct(hv.shape, F32),
        compiler_params=_cparams(2),
        name="merge",
    )(hv, mod, nw.reshape(1, D), wg, gb, *[a for a, _ in ys], rest_out, rest_out, rest_out, *norms, glw, glb, wbr, wo)
    return out.reshape(h.shape)


def _layer_params(l, p):
    offs = [0]
    for w in IN_SPLITS:
        offs.append(offs[-1] + w)
    w_in = p['w_in'][l]
    col = lambda i: w_in[:, offs[i]:offs[i + 1]]
    z128 = lambda k: jnp.zeros((D, k), F32)

    def small(d):
        dt = col(2).reshape(D, 2, SSM_H)[:, d]
        gates = col(6).reshape(D, 2, 2 * ML_H)[:, d]
        ga = col(10).reshape(D, 2, GDN_H)[:, d]
        gb = col(11).reshape(D, 2, GDN_H)[:, d]
        used = SSM_H + 2 * ML_H + 2 * GDN_H
        return jnp.concatenate([dt, gates, ga, gb, z128(LANES - used)], axis=1)

    wc = jnp.concatenate([col(1), col(3), col(8)], axis=1).astype(BF16)
    wr = jnp.concatenate([col(0), col(4), col(5), col(9), small(0), small(1), col(7)], axis=1).astype(BF16)
    cw = jnp.concatenate([p['ssm_conv_w'][l], p['ml_conv_w'][l], p['gdn_conv_w'][l]], axis=1)
    cw = jnp.concatenate([cw, jnp.zeros((SUBLANES - CONV_K, CONV_W), F32)], axis=0)
    cb = jnp.concatenate([p['ssm_conv_b'][l], p['ml_conv_b'][l], jnp.zeros((GDN_H * (2 * GDN_DK + GDN_DV),), F32)])

    def par_rows(d):
        pad = jnp.zeros((LANES - L_GB,), F32)
        bias = jnp.concatenate([p['ssm_dt_bias'][l, d], p['ml_gate_b'][l, d].reshape(-1),
                                p['gdn_dt_bias'][l, d], pad])
        alog = jnp.concatenate([p['ssm_a_log'][l, d], jnp.zeros((2 * ML_H,), F32), p['gdn_a_log'][l, d], pad])
        return jnp.concatenate([bias[None], alog[None], jnp.zeros((SUBLANES - 2, LANES), F32)], axis=0)

    mixpar = jnp.stack([par_rows(0), par_rows(1)])

    def parx_rows(d):
        rows = jnp.stack([jnp.repeat(p['ssm_a_log'][l, d], SSM_P), jnp.repeat(p['ssm_d'][l], SSM_P)])
        return jnp.concatenate([rows, jnp.zeros((SUBLANES - 2, BW), F32)], axis=0)

    ssm_parx = jnp.stack([parx_rows(0), parx_rows(1)])
    wcat, pm, pw = _s5_operators(p['s5_a_re'][l], p['s5_a_im'][l], p['s5_log_dt'][l], p['s5_b_re'][l],
                                 p['s5_b_im'][l], p['s5_c_re'][l], p['s5_c_im'][l])
    return dict(
        wc=wc, wr=wr, cw=cw, cb=cb.reshape(1, CONV_W), mixpar=mixpar, ssm_parx=ssm_parx,
        s5_wcat=wcat, s5_pm=pm, s5_pw=pw, s5_d=p['s5_d'][l].reshape(1, BW),
        wg=col(12).astype(BF16), gb=p['gate_b'][l].reshape(1, N_BRANCH * D),
        norms=(p['ssm_norm'][l].reshape(1, BW), p['ml_norm'][l].reshape(1, BW),
               jnp.tile(p['gdn_norm'][l], GDN_H).reshape(1, BW)),
        glw=p['s5_glu_w'][l].astype(BF16), glb=p['s5_glu_b'][l].reshape(1, BW),
        wbr=p['w_branch'][l].astype(BF16), wo=p['w_out'][l].astype(BF16),
        ffn_up=p['ffn_up'][l].astype(BF16), ffn_dn=p['ffn_down'][l].astype(BF16))


def kernel(x, c, ctx, c_ctx, ada_w, ada_b, norm_w, ffn_up, ffn_down, w_in, ssm_conv_w, ssm_conv_b, ssm_dt_bias, ssm_a_log, ssm_d, ssm_norm, ml_conv_w, ml_conv_b, ml_gate_b, ml_norm, s5_a_re, s5_a_im, s5_log_dt, s5_b_re, s5_b_im, s5_c_re, s5_c_im, s5_d, s5_glu_w, s5_glu_b, gdn_conv_w, gdn_dt_bias, gdn_a_log, gdn_norm, gate_b, w_branch, w_out, final_norm):
    p = dict(ffn_up=ffn_up, ffn_down=ffn_down, w_in=w_in, ssm_conv_w=ssm_conv_w, ssm_conv_b=ssm_conv_b,
             ssm_dt_bias=ssm_dt_bias, ssm_a_log=ssm_a_log, ssm_d=ssm_d, ssm_norm=ssm_norm, ml_conv_w=ml_conv_w,
             ml_conv_b=ml_conv_b, ml_gate_b=ml_gate_b, ml_norm=ml_norm, s5_a_re=s5_a_re, s5_a_im=s5_a_im,
             s5_log_dt=s5_log_dt, s5_b_re=s5_b_re, s5_b_im=s5_b_im, s5_c_re=s5_c_re, s5_c_im=s5_c_im, s5_d=s5_d,
             s5_glu_w=s5_glu_w, s5_glu_b=s5_glu_b, gdn_conv_w=gdn_conv_w, gdn_dt_bias=gdn_dt_bias,
             gdn_a_log=gdn_a_log, gdn_norm=gdn_norm, gate_b=gate_b, w_branch=w_branch, w_out=w_out)
    bsz, n, _ = x.shape
    depth = ada_w.shape[0]
    rows = n // GRID_W
    assert ctx.shape[1] == TM and n % TM == 0 and bsz + 1 <= SUBLANES
    assert TM % rows == 0 and rows % HALO == 0 and n % min(S5_SEG, n) == 0
    cvec = jnp.concatenate([c, c_ctx[None], jnp.zeros((SUBLANES - bsz - 1, D), F32)], axis=0)
    modtab = _mod_table(cvec, ada_w, ada_b)
    lane_head = jnp.arange(LANES)[:, None] == (jnp.arange(BW)[None, :] // SSM_P)
    expand = lane_head.astype(BF16)
    gdn_expand = _gdn_expand()
    h_lat, h_ctx = x, ctx
    nt = n // TM
    for l in range(depth):
        lp = _layer_params(l, p)
        mod = modtab[l]
        need_ctx = l < depth - 1
        col_rows = rows if l % 2 == 1 else None
        h_lat = _ffn(h_lat, mod, None, norm_w[l, 0], lp['ffn_up'][0], lp['ffn_dn'][0], 0)
        h_ctx = _ffn(h_ctx, mod, bsz, norm_w[l, 0], lp['ffn_up'][0], lp['ffn_dn'][0], 0)
        conv_out, rest_out, u_s5 = _inproj(h_lat, h_ctx, mod, norm_w[l, 1], lp['wc'], lp['wr'], lp['cw'],
                                           lp['cb'], col_rows)
        y_ssd = _ssd(conv_out, rest_out, lp['mixpar'], lp['ssm_parx'], expand, n)
        y_ml = _mlstm(conv_out, rest_out, lp['mixpar'], n)
        y_s5_lat, y_s5_ctx = _s5(u_s5, lp['s5_wcat'], lp['s5_pm'], lp['s5_pw'], lp['s5_d'], n)
        y_gdn = _gdn(conv_out, rest_out, lp['mixpar'], gdn_expand, n)
        tail = (rest_out, lp['norms'], lp['glw'], lp['glb'], lp['wbr'], lp['wo'])
        head = (norm_w[l, 1], lp['wg'], lp['gb'])
        ys_lat = ((y_ssd, 0), (y_ml, 0), (y_s5_lat, 0), (y_gdn, 0))
        ys_ctx = ((y_ssd, nt), (y_ml, nt), (y_s5_ctx, 0), (y_gdn, nt))
        h_lat = _merge(h_lat, mod, None, 0, *head, ys_lat, *tail, col_rows)
        last = l == depth - 1
        h_lat = _ffn(h_lat, mod, None, norm_w[l, 2], lp['ffn_up'][1], lp['ffn_dn'][1], 6,
                     final_w=final_norm if last else None)
        if need_ctx:
            h_ctx = _merge(h_ctx, mod, bsz, nt, *head, ys_ctx, *tail, None)
            h_ctx = _ffn(h_ctx, mod, bsz, norm_w[l, 2], lp['ffn_up'][1], lp['ffn_dn'][1], 6)
    return h_lat
```

```python
import functools

import jax
import jax.numpy as jnp
from jax import lax
from jax.experimental import pallas as pl
from jax.experimental.pallas import tpu as pltpu

F32 = jnp.float32
BF16 = jnp.bfloat16

D = 1024
GRID_W = 64
N_MOD = 9
D_FF = 2816
CONV_K = 5
EPS = 1e-6
N_BRANCH = 4
BW = 512
SSM_H, SSM_P, SSM_G, SSM_N, SSM_Q = 8, 64, 2, 128, 128
ML_H, ML_DK, ML_DV, ML_Q = 4, 64, 128, 64
S5_G, S5_GC, S5_P, S5_Q = 32, 16, 64, 16
GDN_H, GDN_DK, GDN_DV, GDN_Q = 4, 128, 128, 64
IN_SPLITS = (512, 1024, 16, 512, 512, 512, 16, 512, 1536, 512, 8, 8, 4096)

LANES = 128
SUBLANES = 8
VMEM_LIMIT = 56 * 1024 * 1024

TM = 256
HALO = SUBLANES
CONV_W = 1024 + 512 + 1536
N_SMALL = 2 * LANES
REST_W = 4 * BW + N_SMALL
REST_COL = dict(ssm_z=0, ml_v=1, ml_o=2, gdn_z=3)
S5_SPLIT = BW // LANES
L_DT, L_IG, L_FG, L_GA, L_GB = 0, 8, 12, 16, 20
S5_SEG = 2048
S5_UNROLL = 4


def _cparams(n_axes):
    return pltpu.CompilerParams(dimension_semantics=("arbitrary",) * n_axes,
                                vmem_limit_bytes=VMEM_LIMIT)


def _const_spec(shape):
    zeros = (0,) * len(shape)
    return pl.BlockSpec(shape, lambda *_: zeros)


def _sigmoid(x):
    return 1.0 / (1.0 + jnp.exp(-x))


def _silu(x):
    return x * _sigmoid(x)


def _softplus(x):
    return jnp.maximum(x, 0.0) + jnp.log1p(jnp.exp(-jnp.abs(x)))


def _gelu_tanh(x):
    return 0.5 * x * (1.0 + jnp.tanh(0.7978845608028654 * (x + 0.044715 * (x * x * x))))


def _rms(x, w):
    return x * lax.rsqrt(jnp.mean(x * x, axis=-1, keepdims=True) + EPS) * w


def _bdot(a, b):
    return jnp.dot(a.astype(BF16), b.astype(BF16), preferred_element_type=F32)


def _bdot_nt(a, b):
    return lax.dot_general(a.astype(BF16), b.astype(BF16), (((1,), (1,)), ((), ())),
                           preferred_element_type=F32)


def _bdot_tn(a, b):
    return lax.dot_general(a.astype(BF16), b.astype(BF16), (((0,), (0,)), ((), ())),
                           preferred_element_type=F32)


def _split3(x):
    hi = x.astype(BF16)
    r = x - hi.astype(F32)
    mid = r.astype(BF16)
    lo = (r - mid.astype(F32)).astype(BF16)
    return hi, mid, lo


def _scan_sel3(q, fwd, all_ones=False):
    ri = lax.broadcasted_iota(jnp.int32, (q, 3 * LANES), 0)
    cj = lax.broadcasted_iota(jnp.int32, (q, 3 * LANES), 1) & (LANES - 1)
    ahead = jnp.zeros_like(ri) if all_ones else (ri - cj) * jnp.where(fwd, 1, -1)
    return jnp.where(jnp.where(cj < q, ahead, -1) >= 0, 1.0, 0.0).astype(BF16)


def _sel_dot_l(sel3, x):
    rows = []
    for p in _split3(x):
        rows.append(p)
        if p.shape[0] < LANES:
            rows.append(jnp.zeros((LANES - p.shape[0], p.shape[1]), BF16))
    return jnp.dot(sel3, jnp.concatenate(rows, axis=0), preferred_element_type=F32)


def _sel_dot_r(x, sel):
    return jnp.dot(jnp.concatenate(_split3(x), axis=1), jnp.concatenate([sel] * 3, axis=0),
                   preferred_element_type=F32)


def _scan_masks(q, fwd):
    ri = lax.broadcasted_iota(jnp.int32, (q, q), 0)
    ci = lax.broadcasted_iota(jnp.int32, (q, q), 1)
    ahead = (ri - ci) * jnp.where(fwd, 1, -1)
    return ahead >= 0, ahead > 0


def _mod_kernel(c_ref, w_ref, b_ref, o_ref):
    s = _silu(c_ref[...])
    o_ref[0, 0] = _bdot(s, w_ref[0]) + b_ref[0, 0]


def _mod_table(cvec, ada_w, ada_b):
    depth = ada_w.shape[0]
    out = pl.pallas_call(
        _mod_kernel,
        grid=(depth, N_MOD),
        in_specs=[_const_spec((SUBLANES, D)),
                  pl.BlockSpec((1, D, D), lambda l, j: (l, 0, j)),
                  pl.BlockSpec((1, 1, 1, D), lambda l, j: (l, j, 0, 0))],
        out_specs=pl.BlockSpec((1, 1, SUBLANES, D), lambda l, j: (l, j, 0, 0)),
        out_shape=jax.ShapeDtypeStruct((depth, N_MOD, SUBLANES, D), F32),
        compiler_params=_cparams(2),
        name="mod_table",
    )(cvec, ada_w, ada_b.reshape(depth, N_MOD, 1, D))
    return jnp.transpose(out, (0, 2, 1, 3))


def _ffn_kernel(h_ref, mod_ref, nw_ref, wup_ref, wdn_ref, *rest, base, final):
    o_ref = rest[-1]
    x = h_ref[0]
    mod = mod_ref[0]
    xm = _rms(x, nw_ref[...]) * (1.0 + mod[base + 1:base + 2]) + mod[base:base + 1]
    xb = xm.astype(BF16)
    g = jnp.dot(xb, wup_ref[:, :D_FF], preferred_element_type=F32)
    u = jnp.dot(xb, wup_ref[:, D_FF:], preferred_element_type=F32)
    y = jnp.dot((_silu(g) * u).astype(BF16), wdn_ref[...], preferred_element_type=F32)
    out = x + (0.5 * mod[base + 2:base + 3]) * y
    if final:
        out = _rms(out, rest[0][...])
    o_ref[0] = out


def _ffn(h, mod, mod_row, nw, w_up, w_dn, base, final_w=None):
    bsz, n, _ = h.shape
    row = (lambda b: b) if mod_row is None else (lambda b: mod_row)
    in_specs = [pl.BlockSpec((1, TM, D), lambda b, t: (b, t, 0)),
                pl.BlockSpec((1, N_MOD, D), lambda b, t: (row(b), 0, 0)),
                _const_spec((1, D)),
                _const_spec((D, 2 * D_FF)),
                _const_spec((D_FF, D))]
    args = [h, mod, nw.reshape(1, D), w_up, w_dn]
    if final_w is not None:
        in_specs.append(_const_spec((1, D)))
        args.append(final_w.reshape(1, D))
    return pl.pallas_call(
        functools.partial(_ffn_kernel, base=base, final=final_w is not None),
        grid=(bsz, n // TM),
        in_specs=in_specs,
        out_specs=pl.BlockSpec((1, TM, D), lambda b, t: (b, t, 0)),
        out_shape=jax.ShapeDtypeStruct(h.shape, F32),
        compiler_params=_cparams(2),
        name="ffn",
    )(*args)


def _lat_tile_spec(n, col_rows):
    if col_rows is None:
        return (1, TM, D), (lambda a: a)
    wpt = TM // col_rows
    return (1, col_rows, wpt * D), (lambda a: a.reshape(a.shape[0], col_rows, GRID_W * D))


def _tile_rows(blk, col_rows):
    if col_rows is None:
        return blk
    wpt = TM // col_rows
    return jnp.concatenate([blk[:, k * D:(k + 1) * D] for k in range(wpt)], axis=0)


def _tile_unrows(x, col_rows):
    if col_rows is None:
        return x
    wpt = TM // col_rows
    return jnp.concatenate([x[k * col_rows:(k + 1) * col_rows] for k in range(wpt)], axis=1)


def _inproj_kernel(mod_ref, hl_ref, hp_ref, hn_ref, hc_ref, nw_ref, wc_ref, wr_ref, cw_ref, cb_ref,
                   oc_ref, or_ref, ou_ref, *, nt, col_rows):
    t = pl.program_id(1)
    is_ctx = t == nt
    mod = mod_ref[0]

    def modulated(x):
        return _rms(x, nw_ref[...]) * (1.0 + mod[4:5]) + mod[3:4]

    has_prev = jnp.logical_and(jnp.logical_not(is_ctx), t >= 1)
    has_next = jnp.logical_and(jnp.logical_not(is_ctx), t <= nt - 2)
    x_prev = modulated(hp_ref[0])
    x_next = modulated(hn_ref[0])
    x_prev = jnp.where(has_prev, x_prev, jnp.zeros_like(x_prev))
    x_next = jnp.where(has_next, x_next, jnp.zeros_like(x_next))
    x_main = modulated(jnp.where(is_ctx, hc_ref[0], _tile_rows(hl_ref[0], col_rows)))
    xb = jnp.concatenate([x_prev, x_main, x_next], axis=0).astype(BF16)
    pc = jnp.dot(xb, wc_ref[...], preferred_element_type=F32)
    acc = cb_ref[...]
    for k in range(CONV_K):
        off = HALO - CONV_K // 2 + k
        acc = acc + cw_ref[k:k + 1, :] * pc[off:off + TM]
    oc_ref[0] = _silu(acc)
    pr = jnp.dot(x_main.astype(BF16), wr_ref[...], preferred_element_type=F32)
    or_ref[0] = pr[:, :REST_W]
    for k in range(S5_SPLIT):
        ou_ref[0, k] = pr[:, REST_W + k * LANES:REST_W + (k + 1) * LANES]


def _inproj(h_lat, h_ctx, mod, nw, wc, wr, cw, cb, col_rows):
    bsz, n, _ = h_lat.shape
    nt = n // TM
    s_tot = n + TM
    tile_shape, view = _lat_tile_spec(n, col_rows)
    lat_t = lambda t: jnp.minimum(t, nt - 1)
    if col_rows is None:
        hb = TM // HALO
        nb = n // HALO
        tile_map = lambda b, t: (b, lat_t(t), 0)
        prev_map = lambda b, t: (b, jnp.clip(t * hb - 1, 0, nb - 1), 0)
        next_map = lambda b, t: (b, jnp.clip((t + 1) * hb, 0, nb - 1), 0)
    else:
        wpt = TM // col_rows
        tile_map = lambda b, t: (b, 0, lat_t(t))
        prev_map = lambda b, t: (b, col_rows // HALO - 1, jnp.clip(t * wpt - 1, 0, GRID_W - 1))
        next_map = lambda b, t: (b, 0, jnp.clip((t + 1) * wpt, 0, GRID_W - 1))
    hv = view(h_lat)
    return pl.pallas_call(
        functools.partial(_inproj_kernel, nt=nt, col_rows=col_rows),
        grid=(bsz, nt + 1),
        in_specs=[pl.BlockSpec((1, N_MOD, D), lambda b, t: (jnp.where(t == nt, bsz, b), 0, 0)),
                  pl.BlockSpec(tile_shape, tile_map),
                  pl.BlockSpec((1, HALO, D), prev_map),
                  pl.BlockSpec((1, HALO, D), next_map),
                  pl.BlockSpec((1, TM, D), lambda b, t: (b, 0, 0)),
                  _const_spec((1, D)),
                  _const_spec((D, CONV_W)),
                  _const_spec((D, REST_W + BW)),
                  _const_spec((SUBLANES, CONV_W)),
                  _const_spec((1, CONV_W))],
        out_specs=[pl.BlockSpec((1, TM, CONV_W), lambda b, t: (b, t, 0)),
                   pl.BlockSpec((1, TM, REST_W), lambda b, t: (b, t, 0)),
                   pl.BlockSpec((1, S5_SPLIT, TM, LANES), lambda b, t: (b, 0, t, 0))],
        out_shape=[jax.ShapeDtypeStruct((bsz, s_tot, CONV_W), F32),
                   jax.ShapeDtypeStruct((bsz, s_tot, REST_W), F32),
                   jax.ShapeDtypeStruct((bsz, S5_SPLIT, s_tot, LANES), F32)],
        compiler_params=_cparams(2),
        name="inproj",
    )(mod, hv, hv, hv, h_ctx, nw.reshape(1, D), wc, wr, cw, cb)


def _chunk_index(n_lat, q):
    ncl, ncc = n_lat // q, TM // q

    def mem_chunk(d, j):
        ctx = jnp.where(d == 0, ncl + j, ncl + ncc - 1 - j)
        lat = jnp.where(d == 0, j - ncc, ncl - 1 - (j - ncc))
        return jnp.where(j < ncc, ctx, lat)
    return mem_chunk, ncl + ncc


def _ssd_kernel(xbc_ref, sm_ref, par_ref, parx_ref, ex_ref, y_ref, h_ref):
    d = pl.program_id(1)

    @pl.when(pl.program_id(2) == 0)
    def _():
        h_ref[...] = jnp.zeros_like(h_ref)

    q = SSM_Q
    fwd = d == 0
    xbc = xbc_ref[0]
    xs = xbc[:, :BW]
    bm = xbc[:, BW:BW + SSM_G * SSM_N].astype(BF16)
    cm = xbc[:, BW + SSM_G * SSM_N:].astype(BF16)
    par = par_ref[0]
    lane = lax.broadcasted_iota(jnp.int32, (1, LANES), 1)
    dt = jnp.where(lane < SSM_H, _softplus(sm_ref[0] + par[0:1]), 0.0)
    da = dt * (-jnp.exp(par[1:2]))
    incl, _ = _scan_masks(q, fwd)
    sel = _scan_sel3(q, fwd)
    g = _sel_dot_l(sel, da)
    g_t = g.T
    dtx = _sel_dot_r(dt, ex_ref[...])
    dax = dtx * (-jnp.exp(parx_ref[0, 0:1]))
    gx = _sel_dot_l(sel, dax)
    glx = jnp.sum(dax, axis=0, keepdims=True)
    xdt = xs * dtx
    xdt_b = xdt.astype(BF16)
    half = (lax.broadcasted_iota(jnp.int32, (1, BW), 1) % (2 * SSM_P)) < SSM_P
    x_even = jnp.where(half, xdt_b, jnp.zeros_like(xdt_b))
    x_odd = jnp.where(half, jnp.zeros_like(xdt_b), xdt_b)
    hpg = SSM_H // SSM_G
    cb = [_bdot_nt(cm[:, k * SSM_N:(k + 1) * SSM_N], bm[:, k * SSM_N:(k + 1) * SSM_N])
          for k in range(SSM_G)]

    def scores(h):
        dec = jnp.exp(jnp.where(incl, g[:, h:h + 1] - g_t[h:h + 1, :], -jnp.inf))
        return (cb[h // hpg] * dec).astype(BF16)

    pairs = []
    for p in range(SSM_H // 2):
        cols = slice(p * 2 * SSM_P, (p + 1) * 2 * SSM_P)
        pairs.append(jnp.dot(scores(2 * p), x_even[:, cols], preferred_element_type=F32)
                     + jnp.dot(scores(2 * p + 1), x_odd[:, cols], preferred_element_type=F32))
    y = jnp.concatenate(pairs, axis=1)
    h_prev = h_ref[...]
    hb = h_prev.astype(BF16)
    gw = hpg * SSM_P
    y_inter = jnp.concatenate(
        [jnp.dot(cm[:, k * SSM_N:(k + 1) * SSM_N], hb[:, k * gw:(k + 1) * gw], preferred_element_type=F32)
         for k in range(SSM_G)], axis=1)
    y = y + y_inter * jnp.exp(gx)
    xw = (xdt * jnp.exp(glx - gx)).astype(BF16)
    st = jnp.concatenate(
        [_bdot_tn(bm[:, k * SSM_N:(k + 1) * SSM_N], xw[:, k * gw:(k + 1) * gw]) for k in range(SSM_G)],
        axis=1)
    h_ref[...] = jnp.exp(glx) * h_prev + st
    skip = jnp.where(fwd, parx_ref[0, 1:2], 0.0)
    y_ref[0, 0] = y + skip * xs


def _ssd(conv_out, rest_out, mixpar, ssm_parx, expand, n_lat):
    bsz, s_tot, _ = conv_out.shape
    mem_chunk, nc = _chunk_index(n_lat, SSM_Q)
    sm_blk = (REST_W - N_SMALL) // LANES
    return pl.pallas_call(
        _ssd_kernel,
        grid=(bsz, 2, nc),
        in_specs=[pl.BlockSpec((1, SSM_Q, 2 * BW), lambda b, d, j: (b, mem_chunk(d, j), 0)),
                  pl.BlockSpec((1, SSM_Q, LANES), lambda b, d, j: (b, mem_chunk(d, j), sm_blk + d)),
                  pl.BlockSpec((1, SUBLANES, LANES), lambda b, d, j: (d, 0, 0)),
                  pl.BlockSpec((1, SUBLANES, BW), lambda b, d, j: (d, 0, 0)),
                  _const_spec((LANES, BW))],
        out_specs=pl.BlockSpec((1, 1, SSM_Q, BW), lambda b, d, j: (d, b, mem_chunk(d, j), 0)),
        out_shape=jax.ShapeDtypeStruct((2, bsz, s_tot, BW), F32),
        scratch_shapes=[pltpu.VMEM((SSM_N, BW), F32)],
        compiler_params=_cparams(3),
        name="ssd",
    )(conv_out, rest_out, mixpar, ssm_parx, expand)


def _mlstm_chunk(qk, v, sm, par, c_prev, m_prev_rows, fwd):
    q, nh = ML_Q, ML_H
    hk = nh * ML_DK
    ew = 2 * ML_DV
    pre = sm + par[0:1]
    lf = -_softplus(-pre)
    incl, _ = _scan_masks(q, fwd)
    bt = _sel_dot_l(_scan_sel3(q, fwd), lf)
    pre_t = pre.T
    bt_t = bt.T
    tot = jnp.sum(lf, axis=0, keepdims=True)
    q_cat = qk[:, :hk] * (ML_DK ** -0.5)
    k_cat = qk[:, hk:]
    ones = jnp.ones((q, ML_DV), F32)
    vext = jnp.concatenate([x for h in range(nh) for x in (v[:, h * ML_DV:(h + 1) * ML_DV], ones)], axis=1)
    vext_bd = _block_diag(vext.astype(BF16), nh)
    s_cat = _bdot_nt(q_cat, _block_diag(k_cat.astype(BF16), nh))
    qc = jnp.dot(q_cat.astype(BF16), _block_diag(c_prev.astype(BF16), nh), preferred_element_type=F32)
    w_parts, kw_parts, rows = [], [], []
    for h in range(nh):
        kh = k_cat[:, h * ML_DK:(h + 1) * ML_DK]
        ig_c, ig_r = pre[:, L_IG + h:L_IG + h + 1], pre_t[L_IG + h:L_IG + h + 1, :]
        bt_c, bt_r = bt[:, L_FG + h:L_FG + h + 1], bt_t[L_FG + h:L_FG + h + 1, :]
        bl = tot[:, L_FG + h:L_FG + h + 1]
        dmat = jnp.where(incl, bt_c - bt_r, -jnp.inf) + ig_r
        m_intra = jnp.max(dmat, axis=1, keepdims=True)
        m_loc = jnp.max(bl - bt_r + ig_r, axis=1, keepdims=True)
        kw_parts.append(kh * jnp.exp(bl - bt_c + ig_c - m_loc))
        m_prev = m_prev_rows[h:h + 1, 0:1]
        e = bt_c + m_prev
        m_out = jnp.maximum(e, m_intra)
        w_intra = jnp.exp(dmat - m_out) * s_cat[:, h * q:(h + 1) * q]
        w_parts.append(w_intra)
        m_new = jnp.maximum(bl + m_prev, m_loc)
        rows.append((jnp.exp(e - m_out), jnp.sum(w_intra, axis=1, keepdims=True), m_out, m_new,
                     jnp.exp(bl + m_prev - m_new), jnp.exp(m_loc - m_new)))
    num_cat = jnp.dot(jnp.concatenate(w_parts, axis=1).astype(BF16), _block_diag(v.astype(BF16), nh),
                      preferred_element_type=F32)
    c_loc = _bdot_tn(jnp.concatenate(kw_parts, axis=0), vext_bd)
    outs, m_rows, keep, gain = [], [], [], []
    for h, (w_x, den_intra, m_out, m_new, s_old, s_loc) in enumerate(rows):
        num = num_cat[:, h * ML_DV:(h + 1) * ML_DV] + w_x * qc[:, h * ew:h * ew + ML_DV]
        den = den_intra + w_x * qc[:, h * ew + ML_DV:h * ew + ML_DV + 1]
        outs.append(num / jnp.maximum(jnp.abs(den), jnp.exp(-m_out)))
        m_rows.append(jnp.broadcast_to(m_new, (1, LANES)))
        keep.append(jnp.broadcast_to(s_old, (1, ew)))
        gain.append(jnp.broadcast_to(s_loc, (1, ew)))
    m_rows.append(jnp.zeros((SUBLANES - nh, LANES), F32))
    c_new = jnp.concatenate(keep, axis=1) * c_prev + jnp.concatenate(gain, axis=1) * c_loc
    return jnp.concatenate(outs, axis=1), c_new, jnp.concatenate(m_rows, axis=0)


def _mlstm_kernel(qk_f_ref, v_f_ref, sm_f_ref, qk_b_ref, v_b_ref, sm_b_ref, par_ref, yf_ref, yb_ref, c_ref,
                  m_ref):
    @pl.when(pl.program_id(1) == 0)
    def _():
        c_ref[...] = jnp.zeros_like(c_ref)
        m_ref[...] = jnp.zeros_like(m_ref)

    y_f, c_f, m_f = _mlstm_chunk(qk_f_ref[0], v_f_ref[0], sm_f_ref[0], par_ref[0], c_ref[0], m_ref[0], True)
    y_b, c_b, m_b = _mlstm_chunk(qk_b_ref[0], v_b_ref[0], sm_b_ref[0], par_ref[1], c_ref[1], m_ref[1], False)
    yf_ref[0] = y_f
    yb_ref[0] = y_b
    c_ref[...] = jnp.stack([c_f, c_b])
    m_ref[...] = jnp.stack([m_f, m_b])


def _mlstm(conv_out, rest_out, mixpar, n_lat):
    bsz, s_tot, _ = conv_out.shape
    sm_blk = (REST_W - N_SMALL) // LANES
    chunk_specs, nc = _bidir_specs(n_lat, ML_Q, [(BW, 2), (BW, REST_COL['ml_v']), (LANES, lambda d: sm_blk + d)])
    out_specs, _ = _bidir_specs(n_lat, ML_Q, [(BW, 0)])
    return pl.pallas_call(
        _mlstm_kernel,
        grid=(bsz, nc),
        in_specs=chunk_specs + [_const_spec(mixpar.shape)],
        out_specs=out_specs,
        out_shape=[jax.ShapeDtypeStruct((bsz, s_tot, BW), F32)] * 2,
        scratch_shapes=[pltpu.VMEM((2, ML_DK, ML_H * 2 * ML_DV), F32),
                        pltpu.VMEM((2, SUBLANES, LANES), F32)],
        compiler_params=_cparams(2),
        name="mlstm",
    )(conv_out, rest_out, rest_out, conv_out, rest_out, rest_out, mixpar)


def _block_diag(x, n_blocks):
    r, cols = x.shape
    w = cols // n_blocks
    tiled = jnp.concatenate([x] * n_blocks, axis=0)
    assert r & (r - 1) == 0 and w & (w - 1) == 0
    ri = lax.shift_right_logical(lax.broadcasted_iota(jnp.int32, tiled.shape, 0), r.bit_length() - 1)
    ci = lax.shift_right_logical(lax.broadcasted_iota(jnp.int32, tiled.shape, 1), w.bit_length() - 1)
    return jnp.where(ri == ci, tiled, jnp.zeros_like(tiled))


def _split_hi_lo(x):
    hi = x.astype(BF16)
    return hi, (x - hi.astype(F32)).astype(BF16)


def _dot_hi_blocks(lhs, rhs_cat, n_blocks):
    lh, ll = _split_hi_lo(lhs)
    rh, rl = _split_hi_lo(rhs_cat)
    rh, rl = _block_diag(rh, n_blocks), _block_diag(rl, n_blocks)
    return jnp.dot(jnp.concatenate([lh, lh, ll], axis=1), jnp.concatenate([rh, rl, rh], axis=0),
                   preferred_element_type=F32)


def _unit_lower_inverse_blocks(l_cat, n_blocks):
    q = l_cat.shape[0]
    ri = lax.broadcasted_iota(jnp.int32, l_cat.shape, 0)
    ci = lax.broadcasted_iota(jnp.int32, l_cat.shape, 1) & (q - 1)
    p = -l_cat
    t = jnp.where(ri == ci, 1.0, 0.0) + p
    p = _dot_hi_blocks(p, p, n_blocks)
    levels = max(q.bit_length() - 2, 0)
    for k in range(levels):
        if k == levels - 1:
            return t + _dot_hi_blocks(t, p, n_blocks)
        both = _dot_hi_blocks(jnp.concatenate([p, t], axis=0), p, n_blocks)
        p, t = both[:q], t + both[q:]
    return t


def _head_l2norm(x, width, scale):
    parts = []
    for h in range(x.shape[1] // width):
        seg = x[:, h * width:(h + 1) * width]
        parts.append(seg * (lax.rsqrt(jnp.sum(seg * seg, axis=-1, keepdims=True) + EPS) * scale))
    return jnp.concatenate(parts, axis=1)


def _gdn_chunk(qkv, sm, par, ex, s_prev, fwd):
    q, nh = GDN_Q, GDN_H
    hk = nh * GDN_DK
    qn = _head_l2norm(qkv[:, :hk], GDN_DK, GDN_DK ** -0.5)
    kn = _head_l2norm(qkv[:, hk:2 * hk], GDN_DK, 1.0)
    v = qkv[:, 2 * hk:]
    lane = lax.broadcasted_iota(jnp.int32, (1, LANES), 1)
    lg = -jnp.exp(par[1:2]) * _softplus(sm + par[0:1])
    g = _sel_dot_l(_scan_sel3(q, fwd), lg)
    packed = jnp.where(lane < L_GB, g, _sigmoid(sm))
    wide = _sel_dot_r(packed, ex)
    g_c = wide[:, :nh * q]
    gx = wide[:, nh * q:nh * q + hk]
    bx = wide[:, nh * q + hk:]
    glx = gx[q - 1:q] if fwd else gx[0:1]
    ri = lax.broadcasted_iota(jnp.int32, (q, nh * q), 0)
    ci = lax.broadcasted_iota(jnp.int32, (q, nh * q), 1) & (q - 1)
    diag = jnp.where(ri == ci, g_c, 0.0)
    g_r = _sel_dot_l(_scan_sel3(q, fwd, all_ones=True), diag)
    ahead = (ri - ci) if fwd else (ci - ri)
    diff = g_c - g_r
    kb = kn * bx
    k_bd = _block_diag(kn.astype(BF16), nh)
    la = _bdot_nt(jnp.concatenate([kb, qn], axis=0), k_bd)
    l_cat = la[:q] * jnp.exp(jnp.where(ahead > 0, diff, -jnp.inf))
    a_qk = la[q:] * jnp.exp(jnp.where(ahead >= 0, diff, -jnp.inf))
    t_inv = _unit_lower_inverse_blocks(l_cat, nh)
    eg = jnp.exp(gx)
    rhs = jnp.concatenate([_block_diag((v * bx).astype(BF16), nh), _block_diag((kb * eg).astype(BF16), nh)], axis=1)
    uw = jnp.dot(t_inv.astype(BF16), rhs, preferred_element_type=F32)
    ws = jnp.dot(jnp.concatenate([uw[:, hk:], qn * eg], axis=0).astype(BF16),
                 _block_diag(s_prev.astype(BF16), nh), preferred_element_type=F32)
    u_bd = _block_diag((uw[:, :hk] - ws[:q]).astype(BF16), nh)
    y = ws[q:] + jnp.dot(a_qk.astype(BF16), u_bd, preferred_element_type=F32)
    kd = (kn * jnp.exp(glx - gx)).astype(BF16)
    kd_rows = jnp.concatenate([kd[:, h * GDN_DK:(h + 1) * GDN_DK] for h in range(nh)], axis=0)
    return y, jnp.exp(glx) * s_prev + _bdot_tn(kd_rows, u_bd)


def _gdn_kernel(qkv_f_ref, sm_f_ref, qkv_b_ref, sm_b_ref, par_ref, ex_ref, yf_ref, yb_ref, s_ref):
    @pl.when(pl.program_id(1) == 0)
    def _():
        s_ref[...] = jnp.zeros_like(s_ref)

    ex = ex_ref[...]
    y_f, s_f = _gdn_chunk(qkv_f_ref[0], sm_f_ref[0], par_ref[0], ex, s_ref[0], True)
    y_b, s_b = _gdn_chunk(qkv_b_ref[0], sm_b_ref[0], par_ref[1], ex, s_ref[1], False)
    yf_ref[0] = y_f
    yb_ref[0] = y_b
    s_ref[...] = jnp.stack([s_f, s_b])


def _gdn_expand():
    src = jnp.arange(LANES)[:, None]
    over_q = src == L_GA + jnp.arange(GDN_H * GDN_Q)[None, :] // GDN_Q
    over_dk = jnp.arange(GDN_H * GDN_DK)[None, :] // GDN_DK
    return jnp.concatenate([over_q, src == L_GA + over_dk, src == L_GB + over_dk], axis=1).astype(BF16)


def _bidir_specs(n_lat, q, blocks):
    mem_chunk, nc = _chunk_index(n_lat, q)
    specs = []
    for d in range(2):
        for width, col in blocks:
            c = col(d) if callable(col) else col
            specs.append(pl.BlockSpec((1, q, width), lambda b, j, d=d, c=c: (b, mem_chunk(d, j), c)))
    return specs, nc


def _gdn(conv_out, rest_out, mixpar, expand, n_lat):
    bsz, s_tot, _ = conv_out.shape
    sm_blk = (REST_W - N_SMALL) // LANES
    chunk_specs, nc = _bidir_specs(n_lat, GDN_Q, [(3 * BW, 1), (LANES, lambda d: sm_blk + d)])
    out_specs, _ = _bidir_specs(n_lat, GDN_Q, [(BW, 0)])
    return pl.pallas_call(
        _gdn_kernel,
        grid=(bsz, nc),
        in_specs=chunk_specs + [_const_spec(mixpar.shape), _const_spec(expand.shape)],
        out_specs=out_specs,
        out_shape=[jax.ShapeDtypeStruct((bsz, s_tot, BW), F32)] * 2,
        scratch_shapes=[pltpu.VMEM((2, GDN_DK, GDN_H * GDN_DV), F32)],
        compiler_params=_cparams(2),
        name="gdn",
    )(conv_out, rest_out, conv_out, rest_out, mixpar, expand)


def _s5_scan(sloc, h_init, pw, fwd, n_valid):
    n = sloc.shape[0]
    row = lax.broadcasted_iota(jnp.int32, (n, 1), 0)

    def mul(x, i):
        return pw[2 * i:2 * i + 1] * x + pw[2 * i + 1:2 * i + 2] * pltpu.roll(x, S5_P, axis=1)

    pos = jnp.where(fwd, row, n - 1 - row)
    first = jnp.where(fwd, 0, n_valid - 1)
    z = pltpu.roll(sloc, jnp.where(fwd, 1, n - 1), axis=0)
    z = jnp.where(pos >= jnp.where(fwd, 1, n - n_valid + 1), z, 0.0)
    z = jnp.where(row == first, h_init, z)
    s, i = 1, 0
    while s < n:
        zs = pltpu.roll(z, jnp.where(fwd, s, n - s), axis=0)
        zs = jnp.where(pos >= s, zs, 0.0)
        z = z + mul(zs, i)
        s, i = 2 * s, i + 1
    last = jnp.where(fwd, n_valid - 1, 0)
    pick = row == last
    z_last = jnp.sum(jnp.where(pick, z, 0.0), axis=0, keepdims=True)
    s_last = jnp.sum(jnp.where(pick, sloc, 0.0), axis=0, keepdims=True)
    return z, mul(z_last, 0) + s_last


def _s5_segment(load_rows, store_rows, n_rows, n_valid, fwd, ut_ref, st_ref, wcat_ref, pm_ref, pw_ref,
                skip_row):
    qg = S5_Q * S5_GC
    for t in range(S5_Q):
        vt = load_rows(t)
        ut_ref[:, t * S5_GC:(t + 1) * S5_GC, :] = vt.T.reshape(S5_G, S5_GC, n_rows)

    def group(gi, carry):
        ug = ut_ref[gi].T[:n_valid].astype(BF16)
        y1 = jnp.dot(ug, wcat_ref[0, gi], preferred_element_type=F32)
        h_start, h_end = _s5_scan(y1[:, qg:], st_ref[gi, 0:1, :], pw_ref[0, gi], fwd, n_valid)
        st_ref[gi, 0:1, :] = h_end
        yy = y1[:, :qg] + jnp.dot(h_start.astype(BF16), pm_ref[0, gi], preferred_element_type=F32)
        if n_valid < n_rows:
            yy = jnp.concatenate([yy, jnp.zeros((n_rows - n_valid, qg), F32)], axis=0)
        ut_ref[gi] = yy.T
        return carry

    lax.fori_loop(0, S5_G, group, 0, unroll=S5_UNROLL)
    for t in range(S5_Q):
        wt = ut_ref[:, t * S5_GC:(t + 1) * S5_GC, :].reshape(BW, n_rows).T
        store_rows(t, wt + skip_row * load_rows(t))


def _load_token(u_ref, t, n_chunks):
    return jnp.concatenate([u_ref[0, k, pl.ds(t, n_chunks, stride=S5_Q), :] for k in range(S5_SPLIT)], axis=1)


def _store_token(y_ref, t, n_chunks, val):
    for k in range(S5_SPLIT):
        y_ref[0, 0, k, pl.ds(t, n_chunks, stride=S5_Q), :] = val[:, k * LANES:(k + 1) * LANES]


def _s5_kernel(ul_ref, uc_ref, wcat_ref, pm_ref, pw_ref, dsk_ref, yl_ref, yc_ref, ut_ref, utc_ref, st_ref,
               *, n_seg_chunks):
    d = pl.program_id(1)
    fwd = d == 0
    skip_row = jnp.where(fwd, dsk_ref[...], 0.0)
    ncc = TM // S5_Q

    @pl.when(pl.program_id(2) == 0)
    def _():
        st_ref[...] = jnp.zeros_like(st_ref)
        pad = jnp.zeros((LANES - ncc, BW), F32)

        def load_c(t):
            return jnp.concatenate([_load_token(uc_ref, t, ncc), pad], axis=0)

        def store_c(t, val):
            _store_token(yc_ref, t, ncc, val[:ncc])

        _s5_segment(load_c, store_c, LANES, ncc, fwd, utc_ref, st_ref, wcat_ref, pm_ref, pw_ref, skip_row)

    def load_l(t):
        return _load_token(ul_ref, t, n_seg_chunks)

    def store_l(t, val):
        _store_token(yl_ref, t, n_seg_chunks, val)

    _s5_segment(load_l, store_l, n_seg_chunks, n_seg_chunks, fwd, ut_ref, st_ref, wcat_ref, pm_ref, pw_ref,
                skip_row)


def _s5(u_slabs, wcat, pm, pw, dskip, n_lat):
    bsz = u_slabs.shape[0]
    seg = min(S5_SEG, n_lat)
    nseg = n_lat // seg
    nsc = seg // S5_Q
    qg = S5_Q * S5_GC
    seg_mem = lambda d, s: jnp.where(d == 0, s, nseg - 1 - s)
    ctx_blk = n_lat // TM
    yl, yc = pl.pallas_call(
        functools.partial(_s5_kernel, n_seg_chunks=nsc),
        grid=(bsz, 2, nseg),
        in_specs=[pl.BlockSpec((1, S5_SPLIT, seg, LANES), lambda b, d, s: (b, 0, seg_mem(d, s), 0)),
                  pl.BlockSpec((1, S5_SPLIT, TM, LANES), lambda b, d, s: (b, 0, ctx_blk, 0)),
                  pl.BlockSpec((1, S5_G, qg, qg + 2 * S5_P), lambda b, d, s: (d, 0, 0, 0)),
                  pl.BlockSpec((1, S5_G, 2 * S5_P, qg), lambda b, d, s: (d, 0, 0, 0)),
                  pl.BlockSpec((1, S5_G, 2 * SUBLANES, LANES), lambda b, d, s: (d, 0, 0, 0)),
                  _const_spec((1, BW))],
        out_specs=[pl.BlockSpec((1, 1, S5_SPLIT, seg, LANES), lambda b, d, s: (d, b, 0, seg_mem(d, s), 0)),
                   pl.BlockSpec((1, 1, S5_SPLIT, TM, LANES), lambda b, d, s: (d, b, 0, 0, 0))],
        out_shape=[jax.ShapeDtypeStruct((2, bsz, S5_SPLIT, n_lat, LANES), F32),
                   jax.ShapeDtypeStruct((2, bsz, S5_SPLIT, TM, LANES), F32)],
        scratch_shapes=[pltpu.VMEM((S5_G, qg, nsc), F32),
                        pltpu.VMEM((S5_G, qg, LANES), F32),
                        pltpu.VMEM((S5_G, SUBLANES, LANES), F32)],
        compiler_params=_cparams(3),
        name="s5",
    )(u_slabs, u_slabs, wcat, pm, pw, dskip)
    return yl, yc


def _s5_operators(a_re, a_im, log_dt, b_re, b_im, c_re, c_im):
    hi = lax.Precision.HIGHEST
    q, nq = S5_Q, S5_Q + 1
    lre = jnp.minimum(a_re.astype(F32), -1e-4)
    lim = a_im.astype(F32)
    dt = jnp.exp(log_dt.astype(F32))[..., None]
    mag = jnp.exp(lre * dt)
    ab_re, ab_im = mag * jnp.cos(lim * dt), mag * jnp.sin(lim * dt)
    den = lre * lre + lim * lim
    f_re = ((ab_re - 1.0) * lre + ab_im * lim) / den
    f_im = (ab_im * lre - (ab_re - 1.0) * lim) / den
    br, bi = b_re.astype(F32), b_im.astype(F32)
    bb_re = f_re[..., None] * br - f_im[..., None] * bi
    bb_im = f_re[..., None] * bi + f_im[..., None] * br
    pr, pi = [jnp.ones_like(ab_re)], [jnp.zeros_like(ab_im)]
    for _ in range(q):
        pr, pi = pr + [pr[-1] * ab_re - pi[-1] * ab_im], pi + [pr[-1] * ab_im + pi[-1] * ab_re]
    pw_re, pw_im = jnp.stack(pr, axis=2), jnp.stack(pi, axis=2)
    e_re = pw_re[..., None] * bb_re[:, :, None] - pw_im[..., None] * bb_im[:, :, None]
    e_im = pw_re[..., None] * bb_im[:, :, None] + pw_im[..., None] * bb_re[:, :, None]
    cr, ci = c_re.astype(F32), c_im.astype(F32)
    kern = (jnp.einsum('gcp,dgnpe->dgnce', cr, e_re, precision=hi)
            - jnp.einsum('gcp,dgnpe->dgnce', ci, e_im, precision=hi))
    s_idx = jnp.arange(q)[:, None]
    t_idx = jnp.arange(q)[None, :]
    mats, rmats, pmats = [], [], []
    for d in range(2):
        lag = (t_idx - s_idx) if d == 0 else (s_idx - t_idx)
        valid = lag >= 0
        m = kern[d][:, jnp.clip(lag, 0, q)]
        m = jnp.where(valid[None, :, :, None, None], m, 0.0)
        mats.append(jnp.transpose(m, (0, 1, 4, 2, 3)).reshape(S5_G, q * S5_GC, q * S5_GC))
        e_s = (q - 1 - jnp.arange(q)) if d == 0 else jnp.arange(q)
        r = jnp.concatenate([e_re[d][:, e_s], e_im[d][:, e_s]], axis=2)
        rmats.append(jnp.transpose(r, (0, 1, 3, 2)).reshape(S5_G, q * S5_GC, 2 * S5_P))
        f_t = (jnp.arange(q) + 1) if d == 0 else (q - jnp.arange(q))
        fr, fi = pw_re[d][:, f_t], pw_im[d][:, f_t]
        p_from_re = cr[:, None] * fr[:, :, None] - ci[:, None] * fi[:, :, None]
        p_from_im = -cr[:, None] * fi[:, :, None] - ci[:, None] * fr[:, :, None]
        p = jnp.concatenate([p_from_re, p_from_im], axis=3)
        pmats.append(jnp.transpose(p, (0, 3, 1, 2)).reshape(S5_G, 2 * S5_P, q * S5_GC))
    wcat = jnp.concatenate([jnp.stack(mats), jnp.stack(rmats)], axis=3).astype(BF16)
    pm = jnp.stack(pmats).astype(BF16)
    dr, di = pw_re[:, :, q], pw_im[:, :, q]
    rows = []
    for _ in range(SUBLANES):
        rows += [jnp.concatenate([dr, dr], axis=-1), jnp.concatenate([-di, di], axis=-1)]
        dr, di = dr * dr - di * di, 2.0 * dr * di
    return wcat, pm, jnp.stack(rows, axis=2)


def _head_rms(x, w, width):
    parts = []
    for h in range(x.shape[1] // width):
        seg = x[:, h * width:(h + 1) * width]
        parts.append(seg * lax.rsqrt(jnp.mean(seg * seg, axis=-1, keepdims=True) + EPS))
    return jnp.concatenate(parts, axis=1) * w


def _merge_kernel(h_ref, mod_ref, nw_ref, wg_ref, gb_ref, ya_ref, ybf_ref, ybb_ref, yc_ref, ydf_ref, ydb_ref,
                  za_ref, ob_ref,
                  zd_ref, na_ref, nb_ref, nd_ref, glw_ref, glb_ref, wbr_ref, wo_ref, o_ref, *, col_rows):
    x = _tile_rows(h_ref[0], col_rows)
    mod = mod_ref[0]
    xb = (_rms(x, nw_ref[...]) * (1.0 + mod[4:5]) + mod[3:4]).astype(BF16)
    gates = _sigmoid(jnp.dot(xb, wg_ref[...], preferred_element_type=F32) + gb_ref[...])
    ya = _rms((ya_ref[0, 0] + ya_ref[1, 0]) * _silu(za_ref[0]), na_ref[...])
    yb = _sigmoid(ob_ref[0]) * _head_rms(ybf_ref[0] + ybb_ref[0], nb_ref[...], ML_DV)
    gc = _gelu_tanh(jnp.concatenate([yc_ref[0, 0, k] + yc_ref[1, 0, k] for k in range(S5_SPLIT)], axis=1))
    yc = gc * _sigmoid(_bdot(gc, glw_ref[...]) + glb_ref[...])
    yd = _head_rms(ydf_ref[0] + ydb_ref[0], nd_ref[...], GDN_DV) * _silu(zd_ref[0])
    merged = jnp.zeros((TM, D), F32)
    for k, yk in enumerate((ya, yb, yc, yd)):
        merged = merged + gates[:, k * D:(k + 1) * D] * _bdot(yk, wbr_ref[k])
    out = _bdot(merged, wo_ref[...])
    o_ref[0] = _tile_unrows(x + mod[5:6] * out, col_rows)


def _merge(h, mod, mod_row, tile0, nw, wg, gb, ys, rest_out, norms, glw, glb, wbr, wo, col_rows):
    bsz, n, _ = h.shape
    nt = n // TM
    tile_shape, view = _lat_tile_spec(n, col_rows)
    tile_map = (lambda b, t: (b, t, 0)) if col_rows is None else (lambda b, t: (b, 0, t))
    row = (lambda b: b) if mod_row is None else (lambda b: mod_row)
    y_spec = lambda t0: pl.BlockSpec((2, 1, TM, BW), lambda b, t: (0, b, t0 + t, 0))
    s5_spec = lambda t0: pl.BlockSpec((2, 1, S5_SPLIT, TM, LANES), lambda b, t: (0, b, 0, t0 + t, 0))
    one_spec = lambda t0: pl.BlockSpec((1, TM, BW), lambda b, t: (b, t0 + t, 0))
    specs = (y_spec, one_spec, one_spec, s5_spec, one_spec, one_spec)
    r_spec = lambda c: pl.BlockSpec((1, TM, BW), lambda b, t: (b, tile0 + t, c))
    hv = view(h)
    out = pl.pallas_call(
        functools.partial(_merge_kernel, col_rows=col_rows),
        grid=(bsz, nt),
        in_specs=[pl.BlockSpec(tile_shape, tile_map),
                  pl.BlockSpec((1, N_MOD, D), lambda b, t: (row(b), 0, 0)),
                  _const_spec((1, D)),
                  _const_spec((D, N_BRANCH * D)),
                  _const_spec((1, N_BRANCH * D)),
                  *[mk(t0) for mk, (_, t0) in zip(specs, ys)],
                  r_spec(REST_COL['ssm_z']), r_spec(REST_COL['ml_o']), r_spec(REST_COL['gdn_z']),
                  _const_spec((1, BW)), _const_spec((1, BW)), _const_spec((1, BW)),
                  _const_spec((BW, BW)), _const_spec((1, BW)),
                  _const_spec((N_BRANCH, BW, D)),
                  _const_spec((D, D))],
        out_specs=pl.BlockSpec(tile_shape, tile_map),
        out_shape=jax.ShapeDtypeStruct(hv.shape, F32),
        compiler_params=_cparams(2),
        name="merge",
    )(hv, mod, nw.reshape(1, D), wg, gb, *[a for a, _ in ys], rest_out, rest_out, rest_out, *norms, glw, glb, wbr, wo)
    return out.reshape(h.shape)


def _layer_params(l, p):
    offs = [0]
    for w in IN_SPLITS:
        offs.append(offs[-1] + w)
    w_in = p['w_in'][l]
    col = lambda i: w_in[:, offs[i]:offs[i + 1]]
    z128 = lambda k: jnp.zeros((D, k), F32)

    def small(d):
        dt = col(2).reshape(D, 2, SSM_H)[:, d]
        gates = col(6).reshape(D, 2, 2 * ML_H)[:, d]
        ga = col(10).reshape(D, 2, GDN_H)[:, d]
        gb = col(11).reshape(D, 2, GDN_H)[:, d]
        used = SSM_H + 2 * ML_H + 2 * GDN_H
        return jnp.concatenate([dt, gates, ga, gb, z128(LANES - used)], axis=1)

    wc = jnp.concatenate([col(1), col(3), col(8)], axis=1).astype(BF16)
    wr = jnp.concatenate([col(0), col(4), col(5), col(9), small(0), small(1), col(7)], axis=1).astype(BF16)
    cw = jnp.concatenate([p['ssm_conv_w'][l], p['ml_conv_w'][l], p['gdn_conv_w'][l]], axis=1)
    cw = jnp.concatenate([cw, jnp.zeros((SUBLANES - CONV_K, CONV_W), F32)], axis=0)
    cb = jnp.concatenate([p['ssm_conv_b'][l], p['ml_conv_b'][l], jnp.zeros((GDN_H * (2 * GDN_DK + GDN_DV),), F32)])

    def par_rows(d):
        pad = jnp.zeros((LANES - L_GB,), F32)
        bias = jnp.concatenate([p['ssm_dt_bias'][l, d], p['ml_gate_b'][l, d].reshape(-1),
                                p['gdn_dt_bias'][l, d], pad])
        alog = jnp.concatenate([p['ssm_a_log'][l, d], jnp.zeros((2 * ML_H,), F32), p['gdn_a_log'][l, d], pad])
        return jnp.concatenate([bias[None], alog[None], jnp.zeros((SUBLANES - 2, LANES), F32)], axis=0)

    mixpar = jnp.stack([par_rows(0), par_rows(1)])

    def parx_rows(d):
        rows = jnp.stack([jnp.repeat(p['ssm_a_log'][l, d], SSM_P), jnp.repeat(p['ssm_d'][l], SSM_P)])
        return jnp.concatenate([rows, jnp.zeros((SUBLANES - 2, BW), F32)], axis=0)

    ssm_parx = jnp.stack([parx_rows(0), parx_rows(1)])
    wcat, pm, pw = _s5_operators(p['s5_a_re'][l], p['s5_a_im'][l], p['s5_log_dt'][l], p['s5_b_re'][l],
                                 p['s5_b_im'][l], p['s5_c_re'][l], p['s5_c_im'][l])
    return dict(
        wc=wc, wr=wr, cw=cw, cb=cb.reshape(1, CONV_W), mixpar=mixpar, ssm_parx=ssm_parx,
        s5_wcat=wcat, s5_pm=pm, s5_pw=pw, s5_d=p['s5_d'][l].reshape(1, BW),
        wg=col(12).astype(BF16), gb=p['gate_b'][l].reshape(1, N_BRANCH * D),
        norms=(p['ssm_norm'][l].reshape(1, BW), p['ml_norm'][l].reshape(1, BW),
               jnp.tile(p['gdn_norm'][l], GDN_H).reshape(1, BW)),
        glw=p['s5_glu_w'][l].astype(BF16), glb=p['s5_glu_b'][l].reshape(1, BW),
        wbr=p['w_branch'][l].astype(BF16), wo=p['w_out'][l].astype(BF16),
        ffn_up=p['ffn_up'][l].astype(BF16), ffn_dn=p['ffn_down'][l].astype(BF16))


def kernel(x, c, ctx, c_ctx, ada_w, ada_b, norm_w, ffn_up, ffn_down, w_in, ssm_conv_w, ssm_conv_b, ssm_dt_bias, ssm_a_log, ssm_d, ssm_norm, ml_conv_w, ml_conv_b, ml_gate_b, ml_norm, s5_a_re, s5_a_im, s5_log_dt, s5_b_re, s5_b_im, s5_c_re, s5_c_im, s5_d, s5_glu_w, s5_glu_b, gdn_conv_w, gdn_dt_bias, gdn_a_log, gdn_norm, gate_b, w_branch, w_out, final_norm):
    p = dict(ffn_up=ffn_up, ffn_down=ffn_down, w_in=w_in, ssm_conv_w=ssm_conv_w, ssm_conv_b=ssm_conv_b,
             ssm_dt_bias=ssm_dt_bias, ssm_a_log=ssm_a_log, ssm_d=ssm_d, ssm_norm=ssm_norm, ml_conv_w=ml_conv_w,
             ml_conv_b=ml_conv_b, ml_gate_b=ml_gate_b, ml_norm=ml_norm, s5_a_re=s5_a_re, s5_a_im=s5_a_im,
             s5_log_dt=s5_log_dt, s5_b_re=s5_b_re, s5_b_im=s5_b_im, s5_c_re=s5_c_re, s5_c_im=s5_c_im, s5_d=s5_d,
             s5_glu_w=s5_glu_w, s5_glu_b=s5_glu_b, gdn_conv_w=gdn_conv_w, gdn_dt_bias=gdn_dt_bias,
             gdn_a_log=gdn_a_log, gdn_norm=gdn_norm, gate_b=gate_b, w_branch=w_branch, w_out=w_out)
    bsz, n, _ = x.shape
    depth = ada_w.shape[0]
    rows = n // GRID_W
    assert ctx.shape[1] == TM and n % TM == 0 and bsz + 1 <= SUBLANES
    assert TM % rows == 0 and rows % HALO == 0 and n % min(S5_SEG, n) == 0
    cvec = jnp.concatenate([c, c_ctx[None], jnp.zeros((SUBLANES - bsz - 1, D), F32)], axis=0)
    modtab = _mod_table(cvec, ada_w, ada_b)
    lane_head = jnp.arange(LANES)[:, None] == (jnp.arange(BW)[None, :] // SSM_P)
    expand = lane_head.astype(BF16)
    gdn_expand = _gdn_expand()
    h_lat, h_ctx = x, ctx
    nt = n // TM
    for l in range(depth):
        lp = _layer_params(l, p)
        mod = modtab[l]
        need_ctx = l < depth - 1
        col_rows = rows if l % 2 == 1 else None
        h_lat = _ffn(h_lat, mod, None, norm_w[l, 0], lp['ffn_up'][0], lp['ffn_dn'][0], 0)
        h_ctx = _ffn(h_ctx, mod, bsz, norm_w[l, 0], lp['ffn_up'][0], lp['ffn_dn'][0], 0)
        conv_out, rest_out, u_s5 = _inproj(h_lat, h_ctx, mod, norm_w[l, 1], lp['wc'], lp['wr'], lp['cw'],
                                           lp['cb'], col_rows)
        y_ssd = _ssd(conv_out, rest_out, lp['mixpar'], lp['ssm_parx'], expand, n)
        y_ml = _mlstm(conv_out, rest_out, lp['mixpar'], n)
        y_s5_lat, y_s5_ctx = _s5(u_s5, lp['s5_wcat'], lp['s5_pm'], lp['s5_pw'], lp['s5_d'], n)
        y_gdn = _gdn(conv_out, rest_out, lp['mixpar'], gdn_expand, n)
        tail = (rest_out, lp['norms'], lp['glw'], lp['glb'], lp['wbr'], lp['wo'])
        head = (norm_w[l, 1], lp['wg'], lp['gb'])
        ys_lat = ((y_ssd, 0), (y_ml[0], 0), (y_ml[1], 0), (y_s5_lat, 0), (y_gdn[0], 0), (y_gdn[1], 0))
        ys_ctx = ((y_ssd, nt), (y_ml[0], nt), (y_ml[1], nt), (y_s5_ctx, 0), (y_gdn[0], nt), (y_gdn[1], nt))
        h_lat = _merge(h_lat, mod, None, 0, *head, ys_lat, *tail, col_rows)
        last = l == depth - 1
        h_lat = _ffn(h_lat, mod, None, norm_w[l, 2], lp['ffn_up'][1], lp['ffn_dn'][1], 6,
                     final_w=final_norm if last else None)
        if need_ctx:
            h_ctx = _merge(h_ctx, mod, bsz, nt, *head, ys_ctx, *tail, None)
            h_ctx = _ffn(h_ctx, mod, bsz, norm_w[l, 2], lp['ffn_up'][1], lp['ffn_dn'][1], 6)
    return h_lat
```

```python
import functools

import jax
import jax.numpy as jnp
from jax import lax
from jax.experimental import pallas as pl
from jax.experimental.pallas import tpu as pltpu

F32 = jnp.float32
BF16 = jnp.bfloat16

D = 1024
GRID_W = 64
N_MOD = 9
D_FF = 2816
CONV_K = 5
EPS = 1e-6
N_BRANCH = 4
BW = 512
SSM_H, SSM_P, SSM_G, SSM_N, SSM_Q = 8, 64, 2, 128, 128
ML_H, ML_DK, ML_DV, ML_Q = 4, 64, 128, 64
S5_G, S5_GC, S5_P, S5_Q = 32, 16, 64, 16
GDN_H, GDN_DK, GDN_DV, GDN_Q = 4, 128, 128, 64
IN_SPLITS = (512, 1024, 16, 512, 512, 512, 16, 512, 1536, 512, 8, 8, 4096)

LANES = 128
SUBLANES = 8
VMEM_LIMIT = 56 * 1024 * 1024

TM = 256
FFN_TM = 512
HALO = SUBLANES
CONV_W = 1024 + 512 + 1536
N_SMALL = 2 * LANES
REST_W = 4 * BW + N_SMALL
REST_COL = dict(ssm_z=0, ml_v=1, ml_o=2, gdn_z=3)
S5_SPLIT = BW // LANES
L_DT, L_IG, L_FG, L_GA, L_GB = 0, 8, 12, 16, 20
S5_SEG = 2048
S5_UNROLL = 4


def _cparams(n_axes):
    return pltpu.CompilerParams(dimension_semantics=("arbitrary",) * n_axes,
                                vmem_limit_bytes=VMEM_LIMIT)


def _const_spec(shape, single_buffer=False):
    zeros = (0,) * len(shape)
    if single_buffer:
        return pl.BlockSpec(shape, lambda *_: zeros, pipeline_mode=pl.Buffered(1))
    return pl.BlockSpec(shape, lambda *_: zeros)


def _sigmoid(x):
    return 1.0 / (1.0 + jnp.exp(-x))


def _silu(x):
    return x * _sigmoid(x)


def _softplus(x):
    return jnp.maximum(x, 0.0) + jnp.log1p(jnp.exp(-jnp.abs(x)))


def _gelu_tanh(x):
    return 0.5 * x * (1.0 + jnp.tanh(0.7978845608028654 * (x + 0.044715 * (x * x * x))))


def _rms(x, w):
    return x * lax.rsqrt(jnp.mean(x * x, axis=-1, keepdims=True) + EPS) * w


def _bdot(a, b):
    return jnp.dot(a.astype(BF16), b.astype(BF16), preferred_element_type=F32)


def _bdot_nt(a, b):
    return lax.dot_general(a.astype(BF16), b.astype(BF16), (((1,), (1,)), ((), ())),
                           preferred_element_type=F32)


def _bdot_tn(a, b):
    return lax.dot_general(a.astype(BF16), b.astype(BF16), (((0,), (0,)), ((), ())),
                           preferred_element_type=F32)


def _split3(x):
    hi = x.astype(BF16)
    r = x - hi.astype(F32)
    mid = r.astype(BF16)
    lo = (r - mid.astype(F32)).astype(BF16)
    return hi, mid, lo


def _scan_sel3(q, fwd, all_ones=False):
    ri = lax.broadcasted_iota(jnp.int32, (q, 3 * LANES), 0)
    cj = lax.broadcasted_iota(jnp.int32, (q, 3 * LANES), 1) & (LANES - 1)
    ahead = jnp.zeros_like(ri) if all_ones else (ri - cj) * jnp.where(fwd, 1, -1)
    return jnp.where(jnp.where(cj < q, ahead, -1) >= 0, 1.0, 0.0).astype(BF16)


def _sel_dot_l(sel3, x):
    rows = []
    for p in _split3(x):
        rows.append(p)
        if p.shape[0] < LANES:
            rows.append(jnp.zeros((LANES - p.shape[0], p.shape[1]), BF16))
    return jnp.dot(sel3, jnp.concatenate(rows, axis=0), preferred_element_type=F32)


def _sel_dot_r(x, sel):
    return jnp.dot(jnp.concatenate(_split3(x), axis=1), jnp.concatenate([sel] * 3, axis=0),
                   preferred_element_type=F32)


def _scan_masks(q, fwd):
    ri = lax.broadcasted_iota(jnp.int32, (q, q), 0)
    ci = lax.broadcasted_iota(jnp.int32, (q, q), 1)
    ahead = (ri - ci) * jnp.where(fwd, 1, -1)
    return ahead >= 0, ahead > 0


def _mod_kernel(c_ref, w_ref, b_ref, o_ref):
    s = _silu(c_ref[...])
    o_ref[0, 0] = _bdot(s, w_ref[0]) + b_ref[0, 0]


def _mod_table(cvec, ada_w, ada_b):
    depth = ada_w.shape[0]
    out = pl.pallas_call(
        _mod_kernel,
        grid=(depth, N_MOD),
        in_specs=[_const_spec((SUBLANES, D)),
                  pl.BlockSpec((1, D, D), lambda l, j: (l, 0, j)),
                  pl.BlockSpec((1, 1, 1, D), lambda l, j: (l, j, 0, 0))],
        out_specs=pl.BlockSpec((1, 1, SUBLANES, D), lambda l, j: (l, j, 0, 0)),
        out_shape=jax.ShapeDtypeStruct((depth, N_MOD, SUBLANES, D), F32),
        compiler_params=_cparams(2),
        name="mod_table",
    )(cvec, ada_w, ada_b.reshape(depth, N_MOD, 1, D))
    return jnp.transpose(out, (0, 2, 1, 3))


def _ffn_kernel(h_ref, mod_ref, nw_ref, wup_ref, wdn_ref, *rest, base, final):
    o_ref = rest[-1]
    x = h_ref[0]
    mod = mod_ref[0]
    xm = _rms(x, nw_ref[...]) * (1.0 + mod[base + 1:base + 2]) + mod[base:base + 1]
    xb = xm.astype(BF16)
    g = jnp.dot(xb, wup_ref[:, :D_FF], preferred_element_type=F32)
    u = jnp.dot(xb, wup_ref[:, D_FF:], preferred_element_type=F32)
    y = jnp.dot((_silu(g) * u).astype(BF16), wdn_ref[...], preferred_element_type=F32)
    out = x + (0.5 * mod[base + 2:base + 3]) * y
    if final:
        out = _rms(out, rest[0][...])
    o_ref[0] = out


def _ffn(h, mod, mod_row, nw, w_up, w_dn, base, final_w=None):
    bsz, n, _ = h.shape
    row = (lambda b: b) if mod_row is None else (lambda b: mod_row)
    tm = FFN_TM if n % FFN_TM == 0 else TM
    in_specs = [pl.BlockSpec((1, tm, D), lambda b, t: (b, t, 0)),
                pl.BlockSpec((1, N_MOD, D), lambda b, t: (row(b), 0, 0)),
                _const_spec((1, D)),
                _const_spec((D, 2 * D_FF), single_buffer=True),
                _const_spec((D_FF, D), single_buffer=True)]
    args = [h, mod, nw.reshape(1, D), w_up, w_dn]
    if final_w is not None:
        in_specs.append(_const_spec((1, D)))
        args.append(final_w.reshape(1, D))
    return pl.pallas_call(
        functools.partial(_ffn_kernel, base=base, final=final_w is not None),
        grid=(bsz, n // tm),
        in_specs=in_specs,
        out_specs=pl.BlockSpec((1, tm, D), lambda b, t: (b, t, 0)),
        out_shape=jax.ShapeDtypeStruct(h.shape, F32),
        compiler_params=_cparams(2),
        name="ffn",
    )(*args)


def _lat_tile_spec(n, col_rows):
    if col_rows is None:
        return (1, TM, D), (lambda a: a)
    wpt = TM // col_rows
    return (1, col_rows, wpt * D), (lambda a: a.reshape(a.shape[0], col_rows, GRID_W * D))


def _tile_rows(blk, col_rows):
    if col_rows is None:
        return blk
    wpt = TM // col_rows
    return jnp.concatenate([blk[:, k * D:(k + 1) * D] for k in range(wpt)], axis=0)


def _tile_unrows(x, col_rows):
    if col_rows is None:
        return x
    wpt = TM // col_rows
    return jnp.concatenate([x[k * col_rows:(k + 1) * col_rows] for k in range(wpt)], axis=1)


def _inproj_kernel(mod_ref, hl_ref, hp_ref, hn_ref, hc_ref, nw_ref, wc_ref, wr_ref, cw_ref, cb_ref,
                   oc_ref, or_ref, ou_ref, *, nt, col_rows):
    t = pl.program_id(1)
    is_ctx = t == nt
    mod = mod_ref[0]

    def modulated(x):
        return _rms(x, nw_ref[...]) * (1.0 + mod[4:5]) + mod[3:4]

    has_prev = jnp.logical_and(jnp.logical_not(is_ctx), t >= 1)
    has_next = jnp.logical_and(jnp.logical_not(is_ctx), t <= nt - 2)
    x_prev = modulated(hp_ref[0])
    x_next = modulated(hn_ref[0])
    x_prev = jnp.where(has_prev, x_prev, jnp.zeros_like(x_prev))
    x_next = jnp.where(has_next, x_next, jnp.zeros_like(x_next))
    x_main = modulated(jnp.where(is_ctx, hc_ref[0], _tile_rows(hl_ref[0], col_rows)))
    xb = jnp.concatenate([x_prev, x_main, x_next], axis=0).astype(BF16)
    pc = jnp.dot(xb, wc_ref[...], preferred_element_type=F32)
    acc = cb_ref[...]
    for k in range(CONV_K):
        off = HALO - CONV_K // 2 + k
        acc = acc + cw_ref[k:k + 1, :] * pc[off:off + TM]
    oc_ref[0] = _silu(acc)
    pr = jnp.dot(x_main.astype(BF16), wr_ref[...], preferred_element_type=F32)
    or_ref[0] = pr[:, :REST_W]
    for k in range(S5_SPLIT):
        ou_ref[0, k] = pr[:, REST_W + k * LANES:REST_W + (k + 1) * LANES]


def _inproj(h_lat, h_ctx, mod, nw, wc, wr, cw, cb, col_rows):
    bsz, n, _ = h_lat.shape
    nt = n // TM
    s_tot = n + TM
    tile_shape, view = _lat_tile_spec(n, col_rows)
    lat_t = lambda t: jnp.minimum(t, nt - 1)
    if col_rows is None:
        hb = TM // HALO
        nb = n // HALO
        tile_map = lambda b, t: (b, lat_t(t), 0)
        prev_map = lambda b, t: (b, jnp.clip(t * hb - 1, 0, nb - 1), 0)
        next_map = lambda b, t: (b, jnp.clip((t + 1) * hb, 0, nb - 1), 0)
    else:
        wpt = TM // col_rows
        tile_map = lambda b, t: (b, 0, lat_t(t))
        prev_map = lambda b, t: (b, col_rows // HALO - 1, jnp.clip(t * wpt - 1, 0, GRID_W - 1))
        next_map = lambda b, t: (b, 0, jnp.clip((t + 1) * wpt, 0, GRID_W - 1))
    hv = view(h_lat)
    return pl.pallas_call(
        functools.partial(_inproj_kernel, nt=nt, col_rows=col_rows),
        grid=(bsz, nt + 1),
        in_specs=[pl.BlockSpec((1, N_MOD, D), lambda b, t: (jnp.where(t == nt, bsz, b), 0, 0)),
                  pl.BlockSpec(tile_shape, tile_map),
                  pl.BlockSpec((1, HALO, D), prev_map),
                  pl.BlockSpec((1, HALO, D), next_map),
                  pl.BlockSpec((1, TM, D), lambda b, t: (b, 0, 0)),
                  _const_spec((1, D)),
                  _const_spec((D, CONV_W)),
                  _const_spec((D, REST_W + BW)),
                  _const_spec((SUBLANES, CONV_W)),
                  _const_spec((1, CONV_W))],
        out_specs=[pl.BlockSpec((1, TM, CONV_W), lambda b, t: (b, t, 0)),
                   pl.BlockSpec((1, TM, REST_W), lambda b, t: (b, t, 0)),
                   pl.BlockSpec((1, S5_SPLIT, TM, LANES), lambda b, t: (b, 0, t, 0))],
        out_shape=[jax.ShapeDtypeStruct((bsz, s_tot, CONV_W), F32),
                   jax.ShapeDtypeStruct((bsz, s_tot, REST_W), F32),
                   jax.ShapeDtypeStruct((bsz, S5_SPLIT, s_tot, LANES), F32)],
        compiler_params=_cparams(2),
        name="inproj",
    )(mod, hv, hv, hv, h_ctx, nw.reshape(1, D), wc, wr, cw, cb)


def _chunk_index(n_lat, q):
    ncl, ncc = n_lat // q, TM // q

    def mem_chunk(d, j):
        ctx = jnp.where(d == 0, ncl + j, ncl + ncc - 1 - j)
        lat = jnp.where(d == 0, j - ncc, ncl - 1 - (j - ncc))
        return jnp.where(j < ncc, ctx, lat)
    return mem_chunk, ncl + ncc


def _ssd_kernel(xbc_ref, sm_ref, par_ref, parx_ref, ex_ref, y_ref, h_ref):
    d = pl.program_id(1)

    @pl.when(pl.program_id(2) == 0)
    def _():
        h_ref[...] = jnp.zeros_like(h_ref)

    q = SSM_Q
    fwd = d == 0
    xbc = xbc_ref[0]
    xs = xbc[:, :BW]
    bm = xbc[:, BW:BW + SSM_G * SSM_N].astype(BF16)
    cm = xbc[:, BW + SSM_G * SSM_N:].astype(BF16)
    par = par_ref[0]
    lane = lax.broadcasted_iota(jnp.int32, (1, LANES), 1)
    dt = jnp.where(lane < SSM_H, _softplus(sm_ref[0] + par[0:1]), 0.0)
    da = dt * (-jnp.exp(par[1:2]))
    incl, _ = _scan_masks(q, fwd)
    sel = _scan_sel3(q, fwd)
    g = _sel_dot_l(sel, da)
    g_t = g.T
    dtx = _sel_dot_r(dt, ex_ref[...])
    dax = dtx * (-jnp.exp(parx_ref[0, 0:1]))
    gx = _sel_dot_l(sel, dax)
    glx = jnp.sum(dax, axis=0, keepdims=True)
    xdt = xs * dtx
    xdt_b = xdt.astype(BF16)
    half = (lax.broadcasted_iota(jnp.int32, (1, BW), 1) % (2 * SSM_P)) < SSM_P
    x_even = jnp.where(half, xdt_b, jnp.zeros_like(xdt_b))
    x_odd = jnp.where(half, jnp.zeros_like(xdt_b), xdt_b)
    hpg = SSM_H // SSM_G
    cb = [_bdot_nt(cm[:, k * SSM_N:(k + 1) * SSM_N], bm[:, k * SSM_N:(k + 1) * SSM_N])
          for k in range(SSM_G)]

    def scores(h):
        dec = jnp.exp(jnp.where(incl, g[:, h:h + 1] - g_t[h:h + 1, :], -jnp.inf))
        return (cb[h // hpg] * dec).astype(BF16)

    pairs = []
    for p in range(SSM_H // 2):
        cols = slice(p * 2 * SSM_P, (p + 1) * 2 * SSM_P)
        pairs.append(jnp.dot(scores(2 * p), x_even[:, cols], preferred_element_type=F32)
                     + jnp.dot(scores(2 * p + 1), x_odd[:, cols], preferred_element_type=F32))
    y = jnp.concatenate(pairs, axis=1)
    h_prev = h_ref[...]
    hb = h_prev.astype(BF16)
    gw = hpg * SSM_P
    y_inter = jnp.concatenate(
        [jnp.dot(cm[:, k * SSM_N:(k + 1) * SSM_N], hb[:, k * gw:(k + 1) * gw], preferred_element_type=F32)
         for k in range(SSM_G)], axis=1)
    y = y + y_inter * jnp.exp(gx)
    xw = (xdt * jnp.exp(glx - gx)).astype(BF16)
    st = jnp.concatenate(
        [_bdot_tn(bm[:, k * SSM_N:(k + 1) * SSM_N], xw[:, k * gw:(k + 1) * gw]) for k in range(SSM_G)],
        axis=1)
    h_ref[...] = jnp.exp(glx) * h_prev + st
    skip = jnp.where(fwd, parx_ref[0, 1:2], 0.0)
    y_ref[0, 0] = y + skip * xs


def _ssd(conv_out, rest_out, mixpar, ssm_parx, expand, n_lat):
    bsz, s_tot, _ = conv_out.shape
    mem_chunk, nc = _chunk_index(n_lat, SSM_Q)
    sm_blk = (REST_W - N_SMALL) // LANES
    return pl.pallas_call(
        _ssd_kernel,
        grid=(bsz, 2, nc),
        in_specs=[pl.BlockSpec((1, SSM_Q, 2 * BW), lambda b, d, j: (b, mem_chunk(d, j), 0)),
                  pl.BlockSpec((1, SSM_Q, LANES), lambda b, d, j: (b, mem_chunk(d, j), sm_blk + d)),
                  pl.BlockSpec((1, SUBLANES, LANES), lambda b, d, j: (d, 0, 0)),
                  pl.BlockSpec((1, SUBLANES, BW), lambda b, d, j: (d, 0, 0)),
                  _const_spec((LANES, BW))],
        out_specs=pl.BlockSpec((1, 1, SSM_Q, BW), lambda b, d, j: (d, b, mem_chunk(d, j), 0)),
        out_shape=jax.ShapeDtypeStruct((2, bsz, s_tot, BW), F32),
        scratch_shapes=[pltpu.VMEM((SSM_N, BW), F32)],
        compiler_params=_cparams(3),
        name="ssd",
    )(conv_out, rest_out, mixpar, ssm_parx, expand)


def _mlstm_chunk(qk, v, sm, par, c_prev, m_prev_rows, fwd):
    q, nh = ML_Q, ML_H
    hk = nh * ML_DK
    ew = 2 * ML_DV
    pre = sm + par[0:1]
    lf = -_softplus(-pre)
    incl, _ = _scan_masks(q, fwd)
    bt = _sel_dot_l(_scan_sel3(q, fwd), lf)
    pre_t = pre.T
    bt_t = bt.T
    tot = jnp.sum(lf, axis=0, keepdims=True)
    q_cat = qk[:, :hk] * (ML_DK ** -0.5)
    k_cat = qk[:, hk:]
    ones = jnp.ones((q, ML_DV), F32)
    vext = jnp.concatenate([x for h in range(nh) for x in (v[:, h * ML_DV:(h + 1) * ML_DV], ones)], axis=1)
    vext_bd = _block_diag(vext.astype(BF16), nh)
    s_cat = _bdot_nt(q_cat, _block_diag(k_cat.astype(BF16), nh))
    qc = jnp.dot(q_cat.astype(BF16), _block_diag(c_prev.astype(BF16), nh), preferred_element_type=F32)
    w_parts, kw_parts, rows = [], [], []
    for h in range(nh):
        kh = k_cat[:, h * ML_DK:(h + 1) * ML_DK]
        ig_c, ig_r = pre[:, L_IG + h:L_IG + h + 1], pre_t[L_IG + h:L_IG + h + 1, :]
        bt_c, bt_r = bt[:, L_FG + h:L_FG + h + 1], bt_t[L_FG + h:L_FG + h + 1, :]
        bl = tot[:, L_FG + h:L_FG + h + 1]
        dmat = jnp.where(incl, bt_c - bt_r, -jnp.inf) + ig_r
        m_intra = jnp.max(dmat, axis=1, keepdims=True)
        m_loc = jnp.max(bl - bt_r + ig_r, axis=1, keepdims=True)
        kw_parts.append(kh * jnp.exp(bl - bt_c + ig_c - m_loc))
        m_prev = m_prev_rows[h:h + 1, 0:1]
        e = bt_c + m_prev
        m_out = jnp.maximum(e, m_intra)
        w_intra = jnp.exp(dmat - m_out) * s_cat[:, h * q:(h + 1) * q]
        w_parts.append(w_intra)
        m_new = jnp.maximum(bl + m_prev, m_loc)
        rows.append((jnp.exp(e - m_out), jnp.sum(w_intra, axis=1, keepdims=True), m_out, m_new,
                     jnp.exp(bl + m_prev - m_new), jnp.exp(m_loc - m_new)))
    num_cat = jnp.dot(jnp.concatenate(w_parts, axis=1).astype(BF16), _block_diag(v.astype(BF16), nh),
                      preferred_element_type=F32)
    c_loc = _bdot_tn(jnp.concatenate(kw_parts, axis=0), vext_bd)
    outs, m_rows, keep, gain = [], [], [], []
    for h, (w_x, den_intra, m_out, m_new, s_old, s_loc) in enumerate(rows):
        num = num_cat[:, h * ML_DV:(h + 1) * ML_DV] + w_x * qc[:, h * ew:h * ew + ML_DV]
        den = den_intra + w_x * qc[:, h * ew + ML_DV:h * ew + ML_DV + 1]
        outs.append(num / jnp.maximum(jnp.abs(den), jnp.exp(-m_out)))
        m_rows.append(jnp.broadcast_to(m_new, (1, LANES)))
        keep.append(jnp.broadcast_to(s_old, (1, ew)))
        gain.append(jnp.broadcast_to(s_loc, (1, ew)))
    m_rows.append(jnp.zeros((SUBLANES - nh, LANES), F32))
    c_new = jnp.concatenate(keep, axis=1) * c_prev + jnp.concatenate(gain, axis=1) * c_loc
    return jnp.concatenate(outs, axis=1), c_new, jnp.concatenate(m_rows, axis=0)


def _mlstm_kernel(qk_f_ref, v_f_ref, sm_f_ref, qk_b_ref, v_b_ref, sm_b_ref, par_ref, yf_ref, yb_ref, c_ref,
                  m_ref):
    @pl.when(pl.program_id(1) == 0)
    def _():
        c_ref[...] = jnp.zeros_like(c_ref)
        m_ref[...] = jnp.zeros_like(m_ref)

    y_f, c_f, m_f = _mlstm_chunk(qk_f_ref[0], v_f_ref[0], sm_f_ref[0], par_ref[0], c_ref[0], m_ref[0], True)
    y_b, c_b, m_b = _mlstm_chunk(qk_b_ref[0], v_b_ref[0], sm_b_ref[0], par_ref[1], c_ref[1], m_ref[1], False)
    yf_ref[0] = y_f
    yb_ref[0] = y_b
    c_ref[...] = jnp.stack([c_f, c_b])
    m_ref[...] = jnp.stack([m_f, m_b])


def _mlstm(conv_out, rest_out, mixpar, n_lat):
    bsz, s_tot, _ = conv_out.shape
    sm_blk = (REST_W - N_SMALL) // LANES
    chunk_specs, nc = _bidir_specs(n_lat, ML_Q, [(BW, 2), (BW, REST_COL['ml_v']), (LANES, lambda d: sm_blk + d)])
    out_specs, _ = _bidir_specs(n_lat, ML_Q, [(BW, 0)])
    return pl.pallas_call(
        _mlstm_kernel,
        grid=(bsz, nc),
        in_specs=chunk_specs + [_const_spec(mixpar.shape)],
        out_specs=out_specs,
        out_shape=[jax.ShapeDtypeStruct((bsz, s_tot, BW), F32)] * 2,
        scratch_shapes=[pltpu.VMEM((2, ML_DK, ML_H * 2 * ML_DV), F32),
                        pltpu.VMEM((2, SUBLANES, LANES), F32)],
        compiler_params=_cparams(2),
        name="mlstm",
    )(conv_out, rest_out, rest_out, conv_out, rest_out, rest_out, mixpar)


def _block_diag(x, n_blocks):
    r, cols = x.shape
    w = cols // n_blocks
    tiled = jnp.concatenate([x] * n_blocks, axis=0)
    assert r & (r - 1) == 0 and w & (w - 1) == 0
    ri = lax.shift_right_logical(lax.broadcasted_iota(jnp.int32, tiled.shape, 0), r.bit_length() - 1)
    ci = lax.shift_right_logical(lax.broadcasted_iota(jnp.int32, tiled.shape, 1), w.bit_length() - 1)
    return jnp.where(ri == ci, tiled, jnp.zeros_like(tiled))


def _split_hi_lo(x):
    hi = x.astype(BF16)
    return hi, (x - hi.astype(F32)).astype(BF16)


def _dot_hi_blocks(lhs, rhs_cat, n_blocks):
    lh, ll = _split_hi_lo(lhs)
    rh, rl = _split_hi_lo(rhs_cat)
    rh, rl = _block_diag(rh, n_blocks), _block_diag(rl, n_blocks)
    return jnp.dot(jnp.concatenate([lh, lh, ll], axis=1), jnp.concatenate([rh, rl, rh], axis=0),
                   preferred_element_type=F32)


def _unit_lower_inverse_blocks(l_cat, n_blocks):
    q = l_cat.shape[0]
    ri = lax.broadcasted_iota(jnp.int32, l_cat.shape, 0)
    ci = lax.broadcasted_iota(jnp.int32, l_cat.shape, 1) & (q - 1)
    p = -l_cat
    t = jnp.where(ri == ci, 1.0, 0.0) + p
    p = _dot_hi_blocks(p, p, n_blocks)
    levels = max(q.bit_length() - 2, 0)
    for k in range(levels):
        if k == levels - 1:
            return t + _dot_hi_blocks(t, p, n_blocks)
        both = _dot_hi_blocks(jnp.concatenate([p, t], axis=0), p, n_blocks)
        p, t = both[:q], t + both[q:]
    return t


def _head_l2norm(x, width, scale):
    parts = []
    for h in range(x.shape[1] // width):
        seg = x[:, h * width:(h + 1) * width]
        parts.append(seg * (lax.rsqrt(jnp.sum(seg * seg, axis=-1, keepdims=True) + EPS) * scale))
    return jnp.concatenate(parts, axis=1)


def _gdn_chunk(qkvs, sms, pars, s_prev, dirs):
    q, nh = GDN_Q, GDN_H
    nb = nh * len(dirs)
    hk = nh * GDN_DK
    cat = lambda parts: jnp.concatenate(parts, axis=1)
    qn = _head_l2norm(cat([x[:, :hk] for x in qkvs]), GDN_DK, GDN_DK ** -0.5)
    kn = _head_l2norm(cat([x[:, hk:2 * hk] for x in qkvs]), GDN_DK, 1.0)
    v = cat([x[:, 2 * hk:] for x in qkvs])
    col = lambda a, lane0, h, w: jnp.broadcast_to(a[:, lane0 + h:lane0 + h + 1], (q, w))
    ri = lax.broadcasted_iota(jnp.int32, (q, nh * q), 0)
    ci = lax.broadcasted_iota(jnp.int32, (q, nh * q), 1) & (q - 1)
    g_c, gx, bx, g_r, glx, ahead = [], [], [], [], [], []
    for sm, par, fwd in zip(sms, pars, dirs):
        lg = -jnp.exp(par[1:2]) * _softplus(sm + par[0:1])
        beta = _sigmoid(sm)
        g = _sel_dot_l(_scan_sel3(q, fwd), lg)
        g_t = g.T
        gx_d = [col(g, L_GA, h, GDN_DK) for h in range(nh)]
        g_c += [col(g, L_GA, h, q) for h in range(nh)]
        gx += gx_d
        bx += [col(beta, L_GB, h, GDN_DK) for h in range(nh)]
        g_r += [g_t[L_GA + h:L_GA + h + 1, :] for h in range(nh)]
        glx += [x[q - 1:q] if fwd else x[0:1] for x in gx_d]
        ahead.append((ri - ci) if fwd else (ci - ri))
    g_c, gx, bx, g_r, glx, ahead = (cat(x) for x in (g_c, gx, bx, g_r, glx, ahead))
    nh = nb
    hk = nb * GDN_DK
    diff = g_c - g_r
    kb = kn * bx
    k_bd = _block_diag(kn.astype(BF16), nh)
    la = _bdot_nt(jnp.concatenate([kb, qn], axis=0), k_bd)
    l_cat = la[:q] * jnp.exp(jnp.where(ahead > 0, diff, -jnp.inf))
    a_qk = la[q:] * jnp.exp(jnp.where(ahead >= 0, diff, -jnp.inf))
    t_inv = _unit_lower_inverse_blocks(l_cat, nh)
    eg = jnp.exp(gx)
    rhs = jnp.concatenate([_block_diag((v * bx).astype(BF16), nh), _block_diag((kb * eg).astype(BF16), nh)], axis=1)
    uw = jnp.dot(t_inv.astype(BF16), rhs, preferred_element_type=F32)
    ws = jnp.dot(jnp.concatenate([uw[:, hk:], qn * eg], axis=0).astype(BF16),
                 _block_diag(s_prev.astype(BF16), nh), preferred_element_type=F32)
    u_bd = _block_diag((uw[:, :hk] - ws[:q]).astype(BF16), nh)
    kd = kn * jnp.exp(glx - gx)
    kd_t = jnp.concatenate([kd[:, h * GDN_DK:(h + 1) * GDN_DK].T for h in range(nh)], axis=1)
    both = jnp.dot(jnp.concatenate([a_qk, kd_t], axis=0).astype(BF16), u_bd, preferred_element_type=F32)
    return ws[q:] + both[:q], jnp.exp(glx) * s_prev + both[q:]


def _gdn_kernel(qkv_f_ref, sm_f_ref, qkv_b_ref, sm_b_ref, par_ref, yf_ref, yb_ref, s_ref):
    @pl.when(pl.program_id(1) == 0)
    def _():
        s_ref[...] = jnp.zeros_like(s_ref)

    y, s_new = _gdn_chunk((qkv_f_ref[0], qkv_b_ref[0]), (sm_f_ref[0], sm_b_ref[0]), (par_ref[0], par_ref[1]),
                          s_ref[...], (True, False))
    yf_ref[0] = y[:, :BW]
    yb_ref[0] = y[:, BW:]
    s_ref[...] = s_new


def _bidir_specs(n_lat, q, blocks):
    mem_chunk, nc = _chunk_index(n_lat, q)
    specs = []
    for d in range(2):
        for width, col in blocks:
            c = col(d) if callable(col) else col
            specs.append(pl.BlockSpec((1, q, width), lambda b, j, d=d, c=c: (b, mem_chunk(d, j), c)))
    return specs, nc


def _gdn(conv_out, rest_out, mixpar, n_lat):
    bsz, s_tot, _ = conv_out.shape
    sm_blk = (REST_W - N_SMALL) // LANES
    chunk_specs, nc = _bidir_specs(n_lat, GDN_Q, [(3 * BW, 1), (LANES, lambda d: sm_blk + d)])
    out_specs, _ = _bidir_specs(n_lat, GDN_Q, [(BW, 0)])
    return pl.pallas_call(
        _gdn_kernel,
        grid=(bsz, nc),
        in_specs=chunk_specs + [_const_spec(mixpar.shape)],
        out_specs=out_specs,
        out_shape=[jax.ShapeDtypeStruct((bsz, s_tot, BW), F32)] * 2,
        scratch_shapes=[pltpu.VMEM((GDN_DK, 2 * GDN_H * GDN_DV), F32)],
        compiler_params=_cparams(2),
        name="gdn",
    )(conv_out, rest_out, conv_out, rest_out, mixpar)


def _s5_scan(sloc, h_init, pw, fwd, n_valid):
    n = sloc.shape[0]
    row = lax.broadcasted_iota(jnp.int32, (n, 1), 0)

    def mul(x, i):
        return pw[2 * i:2 * i + 1] * x + pw[2 * i + 1:2 * i + 2] * pltpu.roll(x, S5_P, axis=1)

    pos = jnp.where(fwd, row, n - 1 - row)
    first = jnp.where(fwd, 0, n_valid - 1)
    z = pltpu.roll(sloc, jnp.where(fwd, 1, n - 1), axis=0)
    z = jnp.where(pos >= jnp.where(fwd, 1, n - n_valid + 1), z, 0.0)
    z = jnp.where(row == first, h_init, z)
    s, i = 1, 0
    while s < n:
        zs = pltpu.roll(z, jnp.where(fwd, s, n - s), axis=0)
        zs = jnp.where(pos >= s, zs, 0.0)
        z = z + mul(zs, i)
        s, i = 2 * s, i + 1
    last = jnp.where(fwd, n_valid - 1, 0)
    pick = row == last
    z_last = jnp.sum(jnp.where(pick, z, 0.0), axis=0, keepdims=True)
    s_last = jnp.sum(jnp.where(pick, sloc, 0.0), axis=0, keepdims=True)
    return z, mul(z_last, 0) + s_last


def _s5_segment(load_rows, store_rows, n_rows, n_valid, fwd, ut_ref, st_ref, wcat_ref, pm_ref, pw_ref,
                skip_row):
    qg = S5_Q * S5_GC
    for t in range(S5_Q):
        vt = load_rows(t)
        ut_ref[:, t * S5_GC:(t + 1) * S5_GC, :] = vt.T.reshape(S5_G, S5_GC, n_rows)

    def group(gi, carry):
        ug = ut_ref[gi].T[:n_valid].astype(BF16)
        y1 = jnp.dot(ug, wcat_ref[0, gi], preferred_element_type=F32)
        h_start, h_end = _s5_scan(y1[:, qg:], st_ref[gi, 0:1, :], pw_ref[0, gi], fwd, n_valid)
        st_ref[gi, 0:1, :] = h_end
        yy = y1[:, :qg] + jnp.dot(h_start.astype(BF16), pm_ref[0, gi], preferred_element_type=F32)
        if n_valid < n_rows:
            yy = jnp.concatenate([yy, jnp.zeros((n_rows - n_valid, qg), F32)], axis=0)
        ut_ref[gi] = yy.T
        return carry

    lax.fori_loop(0, S5_G, group, 0, unroll=S5_UNROLL)
    for t in range(S5_Q):
        wt = ut_ref[:, t * S5_GC:(t + 1) * S5_GC, :].reshape(BW, n_rows).T
        store_rows(t, wt + skip_row * load_rows(t))


def _load_token(u_ref, t, n_chunks):
    return jnp.concatenate([u_ref[0, k, pl.ds(t, n_chunks, stride=S5_Q), :] for k in range(S5_SPLIT)], axis=1)


def _store_token(y_ref, t, n_chunks, val):
    for k in range(S5_SPLIT):
        y_ref[0, 0, k, pl.ds(t, n_chunks, stride=S5_Q), :] = val[:, k * LANES:(k + 1) * LANES]


def _s5_kernel(ul_ref, uc_ref, wcat_ref, pm_ref, pw_ref, dsk_ref, yl_ref, yc_ref, ut_ref, utc_ref, st_ref,
               *, n_seg_chunks):
    d = pl.program_id(1)
    fwd = d == 0
    skip_row = jnp.where(fwd, dsk_ref[...], 0.0)
    ncc = TM // S5_Q

    @pl.when(pl.program_id(2) == 0)
    def _():
        st_ref[...] = jnp.zeros_like(st_ref)
        pad = jnp.zeros((LANES - ncc, BW), F32)

        def load_c(t):
            return jnp.concatenate([_load_token(uc_ref, t, ncc), pad], axis=0)

        def store_c(t, val):
            _store_token(yc_ref, t, ncc, val[:ncc])

        _s5_segment(load_c, store_c, LANES, ncc, fwd, utc_ref, st_ref, wcat_ref, pm_ref, pw_ref, skip_row)

    def load_l(t):
        return _load_token(ul_ref, t, n_seg_chunks)

    def store_l(t, val):
        _store_token(yl_ref, t, n_seg_chunks, val)

    _s5_segment(load_l, store_l, n_seg_chunks, n_seg_chunks, fwd, ut_ref, st_ref, wcat_ref, pm_ref, pw_ref,
                skip_row)


def _s5(u_slabs, wcat, pm, pw, dskip, n_lat):
    bsz = u_slabs.shape[0]
    seg = min(S5_SEG, n_lat)
    nseg = n_lat // seg
    nsc = seg // S5_Q
    qg = S5_Q * S5_GC
    seg_mem = lambda d, s: jnp.where(d == 0, s, nseg - 1 - s)
    ctx_blk = n_lat // TM
    yl, yc = pl.pallas_call(
        functools.partial(_s5_kernel, n_seg_chunks=nsc),
        grid=(bsz, 2, nseg),
        in_specs=[pl.BlockSpec((1, S5_SPLIT, seg, LANES), lambda b, d, s: (b, 0, seg_mem(d, s), 0)),
                  pl.BlockSpec((1, S5_SPLIT, TM, LANES), lambda b, d, s: (b, 0, ctx_blk, 0)),
                  pl.BlockSpec((1, S5_G, qg, qg + 2 * S5_P), lambda b, d, s: (d, 0, 0, 0)),
                  pl.BlockSpec((1, S5_G, 2 * S5_P, qg), lambda b, d, s: (d, 0, 0, 0)),
                  pl.BlockSpec((1, S5_G, 2 * SUBLANES, LANES), lambda b, d, s: (d, 0, 0, 0)),
                  _const_spec((1, BW))],
        out_specs=[pl.BlockSpec((1, 1, S5_SPLIT, seg, LANES), lambda b, d, s: (d, b, 0, seg_mem(d, s), 0)),
                   pl.BlockSpec((1, 1, S5_SPLIT, TM, LANES), lambda b, d, s: (d, b, 0, 0, 0))],
        out_shape=[jax.ShapeDtypeStruct((2, bsz, S5_SPLIT, n_lat, LANES), F32),
                   jax.ShapeDtypeStruct((2, bsz, S5_SPLIT, TM, LANES), F32)],
        scratch_shapes=[pltpu.VMEM((S5_G, qg, nsc), F32),
                        pltpu.VMEM((S5_G, qg, LANES), F32),
                        pltpu.VMEM((S5_G, SUBLANES, LANES), F32)],
        compiler_params=_cparams(3),
        name="s5",
    )(u_slabs, u_slabs, wcat, pm, pw, dskip)
    return yl, yc


def _s5_operators(a_re, a_im, log_dt, b_re, b_im, c_re, c_im):
    hi = lax.Precision.HIGHEST
    q, nq = S5_Q, S5_Q + 1
    lre = jnp.minimum(a_re.astype(F32), -1e-4)
    lim = a_im.astype(F32)
    dt = jnp.exp(log_dt.astype(F32))[..., None]
    mag = jnp.exp(lre * dt)
    ab_re, ab_im = mag * jnp.cos(lim * dt), mag * jnp.sin(lim * dt)
    den = lre * lre + lim * lim
    f_re = ((ab_re - 1.0) * lre + ab_im * lim) / den
    f_im = (ab_im * lre - (ab_re - 1.0) * lim) / den
    br, bi = b_re.astype(F32), b_im.astype(F32)
    bb_re = f_re[..., None] * br - f_im[..., None] * bi
    bb_im = f_re[..., None] * bi + f_im[..., None] * br
    pr, pi = [jnp.ones_like(ab_re)], [jnp.zeros_like(ab_im)]
    for _ in range(q):
        pr, pi = pr + [pr[-1] * ab_re - pi[-1] * ab_im], pi + [pr[-1] * ab_im + pi[-1] * ab_re]
    pw_re, pw_im = jnp.stack(pr, axis=2), jnp.stack(pi, axis=2)
    e_re = pw_re[..., None] * bb_re[:, :, None] - pw_im[..., None] * bb_im[:, :, None]
    e_im = pw_re[..., None] * bb_im[:, :, None] + pw_im[..., None] * bb_re[:, :, None]
    cr, ci = c_re.astype(F32), c_im.astype(F32)
    kern = (jnp.einsum('gcp,dgnpe->dgnce', cr, e_re, precision=hi)
            - jnp.einsum('gcp,dgnpe->dgnce', ci, e_im, precision=hi))
    s_idx = jnp.arange(q)[:, None]
    t_idx = jnp.arange(q)[None, :]
    mats, rmats, pmats = [], [], []
    for d in range(2):
        lag = (t_idx - s_idx) if d == 0 else (s_idx - t_idx)
        valid = lag >= 0
        m = kern[d][:, jnp.clip(lag, 0, q)]
        m = jnp.where(valid[None, :, :, None, None], m, 0.0)
        mats.append(jnp.transpose(m, (0, 1, 4, 2, 3)).reshape(S5_G, q * S5_GC, q * S5_GC))
        e_s = (q - 1 - jnp.arange(q)) if d == 0 else jnp.arange(q)
        r = jnp.concatenate([e_re[d][:, e_s], e_im[d][:, e_s]], axis=2)
        rmats.append(jnp.transpose(r, (0, 1, 3, 2)).reshape(S5_G, q * S5_GC, 2 * S5_P))
        f_t = (jnp.arange(q) + 1) if d == 0 else (q - jnp.arange(q))
        fr, fi = pw_re[d][:, f_t], pw_im[d][:, f_t]
        p_from_re = cr[:, None] * fr[:, :, None] - ci[:, None] * fi[:, :, None]
        p_from_im = -cr[:, None] * fi[:, :, None] - ci[:, None] * fr[:, :, None]
        p = jnp.concatenate([p_from_re, p_from_im], axis=3)
        pmats.append(jnp.transpose(p, (0, 3, 1, 2)).reshape(S5_G, 2 * S5_P, q * S5_GC))
    wcat = jnp.concatenate([jnp.stack(mats), jnp.stack(rmats)], axis=3).astype(BF16)
    pm = jnp.stack(pmats).astype(BF16)
    dr, di = pw_re[:, :, q], pw_im[:, :, q]
    rows = []
    for _ in range(SUBLANES):
        rows += [jnp.concatenate([dr, dr], axis=-1), jnp.concatenate([-di, di], axis=-1)]
        dr, di = dr * dr - di * di, 2.0 * dr * di
    return wcat, pm, jnp.stack(rows, axis=2)


def _head_rms(x, w, width):
    parts = []
    for h in range(x.shape[1] // width):
        seg = x[:, h * width:(h + 1) * width]
        parts.append(seg * lax.rsqrt(jnp.mean(seg * seg, axis=-1, keepdims=True) + EPS))
    return jnp.concatenate(parts, axis=1) * w


def _merge_kernel(h_ref, mod_ref, nw_ref, wg_ref, gb_ref, ya_ref, ybf_ref, ybb_ref, yc_ref, ydf_ref, ydb_ref,
                  za_ref, ob_ref,
                  zd_ref, na_ref, nb_ref, nd_ref, glw_ref, glb_ref, wbr_ref, wo_ref, o_ref, *, col_rows):
    x = _tile_rows(h_ref[0], col_rows)
    mod = mod_ref[0]
    xb = (_rms(x, nw_ref[...]) * (1.0 + mod[4:5]) + mod[3:4]).astype(BF16)
    gates = _sigmoid(jnp.dot(xb, wg_ref[...], preferred_element_type=F32) + gb_ref[...])
    ya = _rms((ya_ref[0, 0] + ya_ref[1, 0]) * _silu(za_ref[0]), na_ref[...])
    yb = _sigmoid(ob_ref[0]) * _head_rms(ybf_ref[0] + ybb_ref[0], nb_ref[...], ML_DV)
    gc = _gelu_tanh(jnp.concatenate([yc_ref[0, 0, k] + yc_ref[1, 0, k] for k in range(S5_SPLIT)], axis=1))
    yc = gc * _sigmoid(_bdot(gc, glw_ref[...]) + glb_ref[...])
    yd = _head_rms(ydf_ref[0] + ydb_ref[0], nd_ref[...], GDN_DV) * _silu(zd_ref[0])
    merged = jnp.zeros((TM, D), F32)
    for k, yk in enumerate((ya, yb, yc, yd)):
        merged = merged + gates[:, k * D:(k + 1) * D] * _bdot(yk, wbr_ref[k])
    out = _bdot(merged, wo_ref[...])
    o_ref[0] = _tile_unrows(x + mod[5:6] * out, col_rows)


def _merge(h, mod, mod_row, tile0, nw, wg, gb, ys, rest_out, norms, glw, glb, wbr, wo, col_rows):
    bsz, n, _ = h.shape
    nt = n // TM
    tile_shape, view = _lat_tile_spec(n, col_rows)
    tile_map = (lambda b, t: (b, t, 0)) if col_rows is None else (lambda b, t: (b, 0, t))
    row = (lambda b: b) if mod_row is None else (lambda b: mod_row)
    y_spec = lambda t0: pl.BlockSpec((2, 1, TM, BW), lambda b, t: (0, b, t0 + t, 0))
    s5_spec = lambda t0: pl.BlockSpec((2, 1, S5_SPLIT, TM, LANES), lambda b, t: (0, b, 0, t0 + t, 0))
    one_spec = lambda t0: pl.BlockSpec((1, TM, BW), lambda b, t: (b, t0 + t, 0))
    specs = (y_spec, one_spec, one_spec, s5_spec, one_spec, one_spec)
    r_spec = lambda c: pl.BlockSpec((1, TM, BW), lambda b, t: (b, tile0 + t, c))
    hv = view(h)
    out = pl.pallas_call(
        functools.partial(_merge_kernel, col_rows=col_rows),
        grid=(bsz, nt),
        in_specs=[pl.BlockSpec(tile_shape, tile_map),
                  pl.BlockSpec((1, N_MOD, D), lambda b, t: (row(b), 0, 0)),
                  _const_spec((1, D)),
                  _const_spec((D, N_BRANCH * D)),
                  _const_spec((1, N_BRANCH * D)),
                  *[mk(t0) for mk, (_, t0) in zip(specs, ys)],
                  r_spec(REST_COL['ssm_z']), r_spec(REST_COL['ml_o']), r_spec(REST_COL['gdn_z']),
                  _const_spec((1, BW)), _const_spec((1, BW)), _const_spec((1, BW)),
                  _const_spec((BW, BW)), _const_spec((1, BW)),
                  _const_spec((N_BRANCH, BW, D)),
                  _const_spec((D, D))],
        out_specs=pl.BlockSpec(tile_shape, tile_map),
        out_shape=jax.ShapeDtypeStruct(hv.shape, F32),
        compiler_params=_cparams(2),
        name="merge",
    )(hv, mod, nw.reshape(1, D), wg, gb, *[a for a, _ in ys], rest_out, rest_out, rest_out, *norms, glw, glb, wbr, wo)
    return out.reshape(h.shape)


def _layer_params(l, p):
    offs = [0]
    for w in IN_SPLITS:
        offs.append(offs[-1] + w)
    w_in = p['w_in'][l]
    col = lambda i: w_in[:, offs[i]:offs[i + 1]]
    z128 = lambda k: jnp.zeros((D, k), F32)

    def small(d):
        dt = col(2).reshape(D, 2, SSM_H)[:, d]
        gates = col(6).reshape(D, 2, 2 * ML_H)[:, d]
        ga = col(10).reshape(D, 2, GDN_H)[:, d]
        gb = col(11).reshape(D, 2, GDN_H)[:, d]
        used = SSM_H + 2 * ML_H + 2 * GDN_H
        return jnp.concatenate([dt, gates, ga, gb, z128(LANES - used)], axis=1)

    wc = jnp.concatenate([col(1), col(3), col(8)], axis=1).astype(BF16)
    wr = jnp.concatenate([col(0), col(4), col(5), col(9), small(0), small(1), col(7)], axis=1).astype(BF16)
    cw = jnp.concatenate([p['ssm_conv_w'][l], p['ml_conv_w'][l], p['gdn_conv_w'][l]], axis=1)
    cw = jnp.concatenate([cw, jnp.zeros((SUBLANES - CONV_K, CONV_W), F32)], axis=0)
    cb = jnp.concatenate([p['ssm_conv_b'][l], p['ml_conv_b'][l], jnp.zeros((GDN_H * (2 * GDN_DK + GDN_DV),), F32)])

    def par_rows(d):
        pad = jnp.zeros((LANES - L_GB,), F32)
        bias = jnp.concatenate([p['ssm_dt_bias'][l, d], p['ml_gate_b'][l, d].reshape(-1),
                                p['gdn_dt_bias'][l, d], pad])
        alog = jnp.concatenate([p['ssm_a_log'][l, d], jnp.zeros((2 * ML_H,), F32), p['gdn_a_log'][l, d], pad])
        return jnp.concatenate([bias[None], alog[None], jnp.zeros((SUBLANES - 2, LANES), F32)], axis=0)

    mixpar = jnp.stack([par_rows(0), par_rows(1)])

    def parx_rows(d):
        rows = jnp.stack([jnp.repeat(p['ssm_a_log'][l, d], SSM_P), jnp.repeat(p['ssm_d'][l], SSM_P)])
        return jnp.concatenate([rows, jnp.zeros((SUBLANES - 2, BW), F32)], axis=0)

    ssm_parx = jnp.stack([parx_rows(0), parx_rows(1)])
    wcat, pm, pw = _s5_operators(p['s5_a_re'][l], p['s5_a_im'][l], p['s5_log_dt'][l], p['s5_b_re'][l],
                                 p['s5_b_im'][l], p['s5_c_re'][l], p['s5_c_im'][l])
    return dict(
        wc=wc, wr=wr, cw=cw, cb=cb.reshape(1, CONV_W), mixpar=mixpar, ssm_parx=ssm_parx,
        s5_wcat=wcat, s5_pm=pm, s5_pw=pw, s5_d=p['s5_d'][l].reshape(1, BW),
        wg=col(12).astype(BF16), gb=p['gate_b'][l].reshape(1, N_BRANCH * D),
        norms=(p['ssm_norm'][l].reshape(1, BW), p['ml_norm'][l].reshape(1, BW),
               jnp.tile(p['gdn_norm'][l], GDN_H).reshape(1, BW)),
        glw=p['s5_glu_w'][l].astype(BF16), glb=p['s5_glu_b'][l].reshape(1, BW),
        wbr=p['w_branch'][l].astype(BF16), wo=p['w_out'][l].astype(BF16),
        ffn_up=p['ffn_up'][l].astype(BF16), ffn_dn=p['ffn_down'][l].astype(BF16))


def kernel(x, c, ctx, c_ctx, ada_w, ada_b, norm_w, ffn_up, ffn_down, w_in, ssm_conv_w, ssm_conv_b, ssm_dt_bias, ssm_a_log, ssm_d, ssm_norm, ml_conv_w, ml_conv_b, ml_gate_b, ml_norm, s5_a_re, s5_a_im, s5_log_dt, s5_b_re, s5_b_im, s5_c_re, s5_c_im, s5_d, s5_glu_w, s5_glu_b, gdn_conv_w, gdn_dt_bias, gdn_a_log, gdn_norm, gate_b, w_branch, w_out, final_norm):
    p = dict(ffn_up=ffn_up, ffn_down=ffn_down, w_in=w_in, ssm_conv_w=ssm_conv_w, ssm_conv_b=ssm_conv_b,
             ssm_dt_bias=ssm_dt_bias, ssm_a_log=ssm_a_log, ssm_d=ssm_d, ssm_norm=ssm_norm, ml_conv_w=ml_conv_w,
             ml_conv_b=ml_conv_b, ml_gate_b=ml_gate_b, ml_norm=ml_norm, s5_a_re=s5_a_re, s5_a_im=s5_a_im,
             s5_log_dt=s5_log_dt, s5_b_re=s5_b_re, s5_b_im=s5_b_im, s5_c_re=s5_c_re, s5_c_im=s5_c_im, s5_d=s5_d,
             s5_glu_w=s5_glu_w, s5_glu_b=s5_glu_b, gdn_conv_w=gdn_conv_w, gdn_dt_bias=gdn_dt_bias,
             gdn_a_log=gdn_a_log, gdn_norm=gdn_norm, gate_b=gate_b, w_branch=w_branch, w_out=w_out)
    bsz, n, _ = x.shape
    depth = ada_w.shape[0]
    rows = n // GRID_W
    assert ctx.shape[1] == TM and n % TM == 0 and bsz + 1 <= SUBLANES
    assert TM % rows == 0 and rows % HALO == 0 and n % min(S5_SEG, n) == 0
    cvec = jnp.concatenate([c, c_ctx[None], jnp.zeros((SUBLANES - bsz - 1, D), F32)], axis=0)
    modtab = _mod_table(cvec, ada_w, ada_b)
    lane_head = jnp.arange(LANES)[:, None] == (jnp.arange(BW)[None, :] // SSM_P)
    expand = lane_head.astype(BF16)
    h_lat, h_ctx = x, ctx
    nt = n // TM
    for l in range(depth):
        lp = _layer_params(l, p)
        mod = modtab[l]
        need_ctx = l < depth - 1
        col_rows = rows if l % 2 == 1 else None
        h_lat = _ffn(h_lat, mod, None, norm_w[l, 0], lp['ffn_up'][0], lp['ffn_dn'][0], 0)
        h_ctx = _ffn(h_ctx, mod, bsz, norm_w[l, 0], lp['ffn_up'][0], lp['ffn_dn'][0], 0)
        conv_out, rest_out, u_s5 = _inproj(h_lat, h_ctx, mod, norm_w[l, 1], lp['wc'], lp['wr'], lp['cw'],
                                           lp['cb'], col_rows)
        y_ssd = _ssd(conv_out, rest_out, lp['mixpar'], lp['ssm_parx'], expand, n)
        y_ml = _mlstm(conv_out, rest_out, lp['mixpar'], n)
        y_s5_lat, y_s5_ctx = _s5(u_s5, lp['s5_wcat'], lp['s5_pm'], lp['s5_pw'], lp['s5_d'], n)
        y_gdn = _gdn(conv_out, rest_out, lp['mixpar'], n)
        tail = (rest_out, lp['norms'], lp['glw'], lp['glb'], lp['wbr'], lp['wo'])
        head = (norm_w[l, 1], lp['wg'], lp['gb'])
        ys_lat = ((y_ssd, 0), (y_ml[0], 0), (y_ml[1], 0), (y_s5_lat, 0), (y_gdn[0], 0), (y_gdn[1], 0))
        ys_ctx = ((y_ssd, nt), (y_ml[0], nt), (y_ml[1], nt), (y_s5_ctx, 0), (y_gdn[0], nt), (y_gdn[1], nt))
        h_lat = _merge(h_lat, mod, None, 0, *head, ys_lat, *tail, col_rows)
        last = l == depth - 1
        h_lat = _ffn(h_lat, mod, None, norm_w[l, 2], lp['ffn_up'][1], lp['ffn_dn'][1], 6,
                     final_w=final_norm if last else None)
        if need_ctx:
            h_ctx = _merge(h_ctx, mod, bsz, nt, *head, ys_ctx, *tail, None)
            h_ctx = _ffn(h_ctx, mod, bsz, norm_w[l, 2], lp['ffn_up'][1], lp['ffn_dn'][1], 6)
    return h_lat
```

```python
import functools

import jax
import jax.numpy as jnp
from jax import lax
from jax.experimental import pallas as pl
from jax.experimental.pallas import tpu as pltpu

F32 = jnp.float32
BF16 = jnp.bfloat16

D = 1024
GRID_W = 64
N_MOD = 9
D_FF = 2816
CONV_K = 5
EPS = 1e-6
N_BRANCH = 4
BW = 512
SSM_H, SSM_P, SSM_G, SSM_N, SSM_Q = 8, 64, 2, 128, 128
ML_H, ML_DK, ML_DV, ML_Q = 4, 64, 128, 64
S5_G, S5_GC, S5_P, S5_Q = 32, 16, 64, 16
GDN_H, GDN_DK, GDN_DV, GDN_Q = 4, 128, 128, 64
IN_SPLITS = (512, 1024, 16, 512, 512, 512, 16, 512, 1536, 512, 8, 8, 4096)

LANES = 128
SUBLANES = 8
VMEM_LIMIT = 56 * 1024 * 1024

TM = 256
FFN_TM = 512
HALO = SUBLANES
CONV_W = 1024 + 512 + 1536
N_SMALL = 2 * LANES
REST_W = 4 * BW + N_SMALL
REST_COL = dict(ssm_z=0, ml_v=1, ml_o=2, gdn_z=3)
S5_SPLIT = BW // LANES
L_DT, L_IG, L_FG, L_GA, L_GB = 0, 8, 12, 16, 20
S5_SEG = 2048
S5_UNROLL = 4


def _cparams(n_axes):
    return pltpu.CompilerParams(dimension_semantics=("arbitrary",) * n_axes,
                                vmem_limit_bytes=VMEM_LIMIT)


def _const_spec(shape, single_buffer=False):
    zeros = (0,) * len(shape)
    if single_buffer:
        return pl.BlockSpec(shape, lambda *_: zeros, pipeline_mode=pl.Buffered(1))
    return pl.BlockSpec(shape, lambda *_: zeros)


def _sigmoid(x):
    return 1.0 / (1.0 + jnp.exp(-x))


def _silu(x):
    return x * _sigmoid(x)


def _softplus(x):
    return jnp.maximum(x, 0.0) + jnp.log1p(jnp.exp(-jnp.abs(x)))


def _gelu_tanh(x):
    return 0.5 * x * (1.0 + jnp.tanh(0.7978845608028654 * (x + 0.044715 * (x * x * x))))


def _rms(x, w):
    return x * lax.rsqrt(jnp.mean(x * x, axis=-1, keepdims=True) + EPS) * w


def _bdot(a, b):
    return jnp.dot(a.astype(BF16), b.astype(BF16), preferred_element_type=F32)


def _bdot_nt(a, b):
    return lax.dot_general(a.astype(BF16), b.astype(BF16), (((1,), (1,)), ((), ())),
                           preferred_element_type=F32)


def _bdot_tn(a, b):
    return lax.dot_general(a.astype(BF16), b.astype(BF16), (((0,), (0,)), ((), ())),
                           preferred_element_type=F32)


def _split3(x):
    hi = x.astype(BF16)
    r = x - hi.astype(F32)
    mid = r.astype(BF16)
    lo = (r - mid.astype(F32)).astype(BF16)
    return hi, mid, lo


def _scan_sel3(q, fwd, all_ones=False):
    ri = lax.broadcasted_iota(jnp.int32, (q, 3 * LANES), 0)
    cj = lax.broadcasted_iota(jnp.int32, (q, 3 * LANES), 1) & (LANES - 1)
    ahead = jnp.zeros_like(ri) if all_ones else (ri - cj) * jnp.where(fwd, 1, -1)
    return jnp.where(jnp.where(cj < q, ahead, -1) >= 0, 1.0, 0.0).astype(BF16)


def _sel_dot_l(sel3, x):
    rows = []
    for p in _split3(x):
        rows.append(p)
        if p.shape[0] < LANES:
            rows.append(jnp.zeros((LANES - p.shape[0], p.shape[1]), BF16))
    return jnp.dot(sel3, jnp.concatenate(rows, axis=0), preferred_element_type=F32)


def _sel_dot_r(x, sel):
    return jnp.dot(jnp.concatenate(_split3(x), axis=1), jnp.concatenate([sel] * 3, axis=0),
                   preferred_element_type=F32)


def _scan_masks(q, fwd):
    ri = lax.broadcasted_iota(jnp.int32, (q, q), 0)
    ci = lax.broadcasted_iota(jnp.int32, (q, q), 1)
    ahead = (ri - ci) * jnp.where(fwd, 1, -1)
    return ahead >= 0, ahead > 0


def _mod_kernel(c_ref, w_ref, b_ref, o_ref):
    s = _silu(c_ref[...])
    o_ref[0, 0] = _bdot(s, w_ref[0]) + b_ref[0, 0]


def _mod_table(cvec, ada_w, ada_b):
    depth = ada_w.shape[0]
    out = pl.pallas_call(
        _mod_kernel,
        grid=(depth, N_MOD),
        in_specs=[_const_spec((SUBLANES, D)),
                  pl.BlockSpec((1, D, D), lambda l, j: (l, 0, j)),
                  pl.BlockSpec((1, 1, 1, D), lambda l, j: (l, j, 0, 0))],
        out_specs=pl.BlockSpec((1, 1, SUBLANES, D), lambda l, j: (l, j, 0, 0)),
        out_shape=jax.ShapeDtypeStruct((depth, N_MOD, SUBLANES, D), F32),
        compiler_params=_cparams(2),
        name="mod_table",
    )(cvec, ada_w, ada_b.reshape(depth, N_MOD, 1, D))
    return jnp.transpose(out, (0, 2, 1, 3))


def _ffn_kernel(h_ref, mod_ref, nw_ref, wup_ref, wdn_ref, *rest, base, final):
    o_ref = rest[-1]
    x = h_ref[0]
    mod = mod_ref[0]
    xm = _rms(x, nw_ref[...]) * (1.0 + mod[base + 1:base + 2]) + mod[base:base + 1]
    xb = xm.astype(BF16)
    g = jnp.dot(xb, wup_ref[:, :D_FF], preferred_element_type=F32)
    u = jnp.dot(xb, wup_ref[:, D_FF:], preferred_element_type=F32)
    y = jnp.dot((_silu(g) * u).astype(BF16), wdn_ref[...], preferred_element_type=F32)
    out = x + (0.5 * mod[base + 2:base + 3]) * y
    if final:
        out = _rms(out, rest[0][...])
    o_ref[0] = out


def _ffn(h, mod, mod_row, nw, w_up, w_dn, base, final_w=None):
    bsz, n, _ = h.shape
    row = (lambda b: b) if mod_row is None else (lambda b: mod_row)
    tm = FFN_TM if n % FFN_TM == 0 else TM
    in_specs = [pl.BlockSpec((1, tm, D), lambda b, t: (b, t, 0)),
                pl.BlockSpec((1, N_MOD, D), lambda b, t: (row(b), 0, 0)),
                _const_spec((1, D)),
                _const_spec((D, 2 * D_FF), single_buffer=True),
                _const_spec((D_FF, D), single_buffer=True)]
    args = [h, mod, nw.reshape(1, D), w_up, w_dn]
    if final_w is not None:
        in_specs.append(_const_spec((1, D)))
        args.append(final_w.reshape(1, D))
    return pl.pallas_call(
        functools.partial(_ffn_kernel, base=base, final=final_w is not None),
        grid=(bsz, n // tm),
        in_specs=in_specs,
        out_specs=pl.BlockSpec((1, tm, D), lambda b, t: (b, t, 0)),
        out_shape=jax.ShapeDtypeStruct(h.shape, F32),
        compiler_params=_cparams(2),
        name="ffn",
    )(*args)


def _lat_tile_spec(n, col_rows):
    if col_rows is None:
        return (1, TM, D), (lambda a: a)
    wpt = TM // col_rows
    return (1, col_rows, wpt * D), (lambda a: a.reshape(a.shape[0], col_rows, GRID_W * D))


def _tile_rows(blk, col_rows):
    if col_rows is None:
        return blk
    wpt = TM // col_rows
    return jnp.concatenate([blk[:, k * D:(k + 1) * D] for k in range(wpt)], axis=0)


def _tile_unrows(x, col_rows):
    if col_rows is None:
        return x
    wpt = TM // col_rows
    return jnp.concatenate([x[k * col_rows:(k + 1) * col_rows] for k in range(wpt)], axis=1)


def _inproj_kernel(mod_ref, hl_ref, hp_ref, hn_ref, hc_ref, nw_ref, wc_ref, wr_ref, cw_ref, cb_ref,
                   oc_ref, or_ref, ou_ref, *, nt, col_rows):
    t = pl.program_id(1)
    is_ctx = t == nt
    mod = mod_ref[0]

    def modulated(x):
        return _rms(x, nw_ref[...]) * (1.0 + mod[4:5]) + mod[3:4]

    has_prev = jnp.logical_and(jnp.logical_not(is_ctx), t >= 1)
    has_next = jnp.logical_and(jnp.logical_not(is_ctx), t <= nt - 2)
    x_prev = modulated(hp_ref[0])
    x_next = modulated(hn_ref[0])
    x_prev = jnp.where(has_prev, x_prev, jnp.zeros_like(x_prev))
    x_next = jnp.where(has_next, x_next, jnp.zeros_like(x_next))
    x_main = modulated(jnp.where(is_ctx, hc_ref[0], _tile_rows(hl_ref[0], col_rows)))
    xb = jnp.concatenate([x_prev, x_main, x_next], axis=0).astype(BF16)
    pc = jnp.dot(xb, wc_ref[...], preferred_element_type=F32)
    acc = cb_ref[...]
    for k in range(CONV_K):
        off = HALO - CONV_K // 2 + k
        acc = acc + cw_ref[k:k + 1, :] * pc[off:off + TM]
    oc_ref[0] = _silu(acc)
    pr = jnp.dot(x_main.astype(BF16), wr_ref[...], preferred_element_type=F32)
    or_ref[0] = pr[:, :REST_W]
    for k in range(S5_SPLIT):
        ou_ref[0, k] = pr[:, REST_W + k * LANES:REST_W + (k + 1) * LANES]


def _inproj(h_lat, h_ctx, mod, nw, wc, wr, cw, cb, col_rows):
    bsz, n, _ = h_lat.shape
    nt = n // TM
    s_tot = n + TM
    tile_shape, view = _lat_tile_spec(n, col_rows)
    lat_t = lambda t: jnp.minimum(t, nt - 1)
    if col_rows is None:
        hb = TM // HALO
        nb = n // HALO
        tile_map = lambda b, t: (b, lat_t(t), 0)
        prev_map = lambda b, t: (b, jnp.clip(t * hb - 1, 0, nb - 1), 0)
        next_map = lambda b, t: (b, jnp.clip((t + 1) * hb, 0, nb - 1), 0)
    else:
        wpt = TM // col_rows
        tile_map = lambda b, t: (b, 0, lat_t(t))
        prev_map = lambda b, t: (b, col_rows // HALO - 1, jnp.clip(t * wpt - 1, 0, GRID_W - 1))
        next_map = lambda b, t: (b, 0, jnp.clip((t + 1) * wpt, 0, GRID_W - 1))
    hv = view(h_lat)
    return pl.pallas_call(
        functools.partial(_inproj_kernel, nt=nt, col_rows=col_rows),
        grid=(bsz, nt + 1),
        in_specs=[pl.BlockSpec((1, N_MOD, D), lambda b, t: (jnp.where(t == nt, bsz, b), 0, 0)),
                  pl.BlockSpec(tile_shape, tile_map),
                  pl.BlockSpec((1, HALO, D), prev_map),
                  pl.BlockSpec((1, HALO, D), next_map),
                  pl.BlockSpec((1, TM, D), lambda b, t: (b, 0, 0)),
                  _const_spec((1, D)),
                  _const_spec((D, CONV_W)),
                  _const_spec((D, REST_W + BW)),
                  _const_spec((SUBLANES, CONV_W)),
                  _const_spec((1, CONV_W))],
        out_specs=[pl.BlockSpec((1, TM, CONV_W), lambda b, t: (b, t, 0)),
                   pl.BlockSpec((1, TM, REST_W), lambda b, t: (b, t, 0)),
                   pl.BlockSpec((1, S5_SPLIT, TM, LANES), lambda b, t: (b, 0, t, 0))],
        out_shape=[jax.ShapeDtypeStruct((bsz, s_tot, CONV_W), F32),
                   jax.ShapeDtypeStruct((bsz, s_tot, REST_W), F32),
                   jax.ShapeDtypeStruct((bsz, S5_SPLIT, s_tot, LANES), F32)],
        compiler_params=_cparams(2),
        name="inproj",
    )(mod, hv, hv, hv, h_ctx, nw.reshape(1, D), wc, wr, cw, cb)


def _chunk_index(n_lat, q):
    ncl, ncc = n_lat // q, TM // q

    def mem_chunk(d, j):
        ctx = jnp.where(d == 0, ncl + j, ncl + ncc - 1 - j)
        lat = jnp.where(d == 0, j - ncc, ncl - 1 - (j - ncc))
        return jnp.where(j < ncc, ctx, lat)
    return mem_chunk, ncl + ncc


def _ssd_chunk(xbc, sm, par, parx, ex, h_prev, fwd):
    q = SSM_Q
    xs = xbc[:, :BW]
    bm = xbc[:, BW:BW + SSM_G * SSM_N].astype(BF16)
    cm = xbc[:, BW + SSM_G * SSM_N:].astype(BF16)
    lane = lax.broadcasted_iota(jnp.int32, (1, LANES), 1)
    dt = jnp.where(lane < SSM_H, _softplus(sm + par[0:1]), 0.0)
    da = dt * (-jnp.exp(par[1:2]))
    incl, _ = _scan_masks(q, fwd)
    sel = _scan_sel3(q, fwd)
    g = _sel_dot_l(sel, da)
    g_t = g.T
    dtx = _sel_dot_r(dt, ex)
    dax = dtx * (-jnp.exp(parx[0:1]))
    gx = _sel_dot_l(sel, dax)
    glx = jnp.sum(dax, axis=0, keepdims=True)
    xdt = xs * dtx
    xdt_b = xdt.astype(BF16)
    half = (lax.broadcasted_iota(jnp.int32, (1, BW), 1) % (2 * SSM_P)) < SSM_P
    x_even = jnp.where(half, xdt_b, jnp.zeros_like(xdt_b))
    x_odd = jnp.where(half, jnp.zeros_like(xdt_b), xdt_b)
    hpg = SSM_H // SSM_G
    cb = [_bdot_nt(cm[:, k * SSM_N:(k + 1) * SSM_N], bm[:, k * SSM_N:(k + 1) * SSM_N])
          for k in range(SSM_G)]

    def scores(h):
        dec = jnp.exp(jnp.where(incl, g[:, h:h + 1] - g_t[h:h + 1, :], -jnp.inf))
        return (cb[h // hpg] * dec).astype(BF16)

    pairs = []
    for p in range(SSM_H // 2):
        cols = slice(p * 2 * SSM_P, (p + 1) * 2 * SSM_P)
        pairs.append(jnp.dot(scores(2 * p), x_even[:, cols], preferred_element_type=F32)
                     + jnp.dot(scores(2 * p + 1), x_odd[:, cols], preferred_element_type=F32))
    y = jnp.concatenate(pairs, axis=1)
    hb = h_prev.astype(BF16)
    gw = hpg * SSM_P
    y_inter = jnp.concatenate(
        [jnp.dot(cm[:, k * SSM_N:(k + 1) * SSM_N], hb[:, k * gw:(k + 1) * gw], preferred_element_type=F32)
         for k in range(SSM_G)], axis=1)
    y = y + y_inter * jnp.exp(gx)
    xw = (xdt * jnp.exp(glx - gx)).astype(BF16)
    st = jnp.concatenate(
        [_bdot_tn(bm[:, k * SSM_N:(k + 1) * SSM_N], xw[:, k * gw:(k + 1) * gw]) for k in range(SSM_G)],
        axis=1)
    if fwd:
        y = y + parx[1:2] * xs
    return y, jnp.exp(glx) * h_prev + st


def _ssd_kernel(xbc_f_ref, sm_f_ref, xbc_b_ref, sm_b_ref, par_ref, parx_ref, ex_ref, yf_ref, yb_ref, h_ref):
    @pl.when(pl.program_id(1) == 0)
    def _():
        h_ref[...] = jnp.zeros_like(h_ref)

    ex = ex_ref[...]
    y_f, h_f = _ssd_chunk(xbc_f_ref[0], sm_f_ref[0], par_ref[0], parx_ref[0], ex, h_ref[0], True)
    y_b, h_b = _ssd_chunk(xbc_b_ref[0], sm_b_ref[0], par_ref[1], parx_ref[1], ex, h_ref[1], False)
    yf_ref[0] = y_f
    yb_ref[0] = y_b
    h_ref[...] = jnp.stack([h_f, h_b])


def _ssd(conv_out, rest_out, mixpar, ssm_parx, expand, n_lat):
    bsz, s_tot, _ = conv_out.shape
    sm_blk = (REST_W - N_SMALL) // LANES
    chunk_specs, nc = _bidir_specs(n_lat, SSM_Q, [(2 * BW, 0), (LANES, lambda d: sm_blk + d)])
    out_specs, _ = _bidir_specs(n_lat, SSM_Q, [(BW, 0)])
    return pl.pallas_call(
        _ssd_kernel,
        grid=(bsz, nc),
        in_specs=chunk_specs + [_const_spec(mixpar.shape), _const_spec(ssm_parx.shape), _const_spec((LANES, BW))],
        out_specs=out_specs,
        out_shape=[jax.ShapeDtypeStruct((bsz, s_tot, BW), F32)] * 2,
        scratch_shapes=[pltpu.VMEM((2, SSM_N, BW), F32)],
        compiler_params=_cparams(2),
        name="ssd",
    )(conv_out, rest_out, conv_out, rest_out, mixpar, ssm_parx, expand)


def _mlstm_chunk(qk, v, sm, par, c_prev, m_prev_rows, fwd):
    q, nh = ML_Q, ML_H
    hk = nh * ML_DK
    ew = 2 * ML_DV
    pre = sm + par[0:1]
    lf = -_softplus(-pre)
    incl, _ = _scan_masks(q, fwd)
    bt = _sel_dot_l(_scan_sel3(q, fwd), lf)
    pre_t = pre.T
    bt_t = bt.T
    tot = jnp.sum(lf, axis=0, keepdims=True)
    q_cat = qk[:, :hk] * (ML_DK ** -0.5)
    k_cat = qk[:, hk:]
    ones = jnp.ones((q, ML_DV), F32)
    vext = jnp.concatenate([x for h in range(nh) for x in (v[:, h * ML_DV:(h + 1) * ML_DV], ones)], axis=1)
    vext_bd = _block_diag(vext.astype(BF16), nh)
    s_cat = _bdot_nt(q_cat, _block_diag(k_cat.astype(BF16), nh))
    qc = jnp.dot(q_cat.astype(BF16), _block_diag(c_prev.astype(BF16), nh), preferred_element_type=F32)
    w_parts, kw_parts, rows = [], [], []
    for h in range(nh):
        kh = k_cat[:, h * ML_DK:(h + 1) * ML_DK]
        ig_c, ig_r = pre[:, L_IG + h:L_IG + h + 1], pre_t[L_IG + h:L_IG + h + 1, :]
        bt_c, bt_r = bt[:, L_FG + h:L_FG + h + 1], bt_t[L_FG + h:L_FG + h + 1, :]
        bl = tot[:, L_FG + h:L_FG + h + 1]
        dmat = jnp.where(incl, bt_c - bt_r, -jnp.inf) + ig_r
        m_intra = jnp.max(dmat, axis=1, keepdims=True)
        m_loc = jnp.max(bl - bt_r + ig_r, axis=1, keepdims=True)
        kw_parts.append(kh * jnp.exp(bl - bt_c + ig_c - m_loc))
        m_prev = m_prev_rows[h:h + 1, 0:1]
        e = bt_c + m_prev
        m_out = jnp.maximum(e, m_intra)
        w_intra = jnp.exp(dmat - m_out) * s_cat[:, h * q:(h + 1) * q]
        w_parts.append(w_intra)
        m_new = jnp.maximum(bl + m_prev, m_loc)
        rows.append((jnp.exp(e - m_out), jnp.sum(w_intra, axis=1, keepdims=True), m_out, m_new,
                     jnp.exp(bl + m_prev - m_new), jnp.exp(m_loc - m_new)))
    num_cat = jnp.dot(jnp.concatenate(w_parts, axis=1).astype(BF16), _block_diag(v.astype(BF16), nh),
                      preferred_element_type=F32)
    c_loc = _bdot_tn(jnp.concatenate(kw_parts, axis=0), vext_bd)
    outs, m_rows, keep, gain = [], [], [], []
    for h, (w_x, den_intra, m_out, m_new, s_old, s_loc) in enumerate(rows):
        num = num_cat[:, h * ML_DV:(h + 1) * ML_DV] + w_x * qc[:, h * ew:h * ew + ML_DV]
        den = den_intra + w_x * qc[:, h * ew + ML_DV:h * ew + ML_DV + 1]
        outs.append(num / jnp.maximum(jnp.abs(den), jnp.exp(-m_out)))
        m_rows.append(jnp.broadcast_to(m_new, (1, LANES)))
        keep.append(jnp.broadcast_to(s_old, (1, ew)))
        gain.append(jnp.broadcast_to(s_loc, (1, ew)))
    m_rows.append(jnp.zeros((SUBLANES - nh, LANES), F32))
    c_new = jnp.concatenate(keep, axis=1) * c_prev + jnp.concatenate(gain, axis=1) * c_loc
    return jnp.concatenate(outs, axis=1), c_new, jnp.concatenate(m_rows, axis=0)


def _mlstm_kernel(qk_f_ref, v_f_ref, sm_f_ref, qk_b_ref, v_b_ref, sm_b_ref, par_ref, yf_ref, yb_ref, c_ref,
                  m_ref):
    @pl.when(pl.program_id(1) == 0)
    def _():
        c_ref[...] = jnp.zeros_like(c_ref)
        m_ref[...] = jnp.zeros_like(m_ref)

    y_f, c_f, m_f = _mlstm_chunk(qk_f_ref[0], v_f_ref[0], sm_f_ref[0], par_ref[0], c_ref[0], m_ref[0], True)
    y_b, c_b, m_b = _mlstm_chunk(qk_b_ref[0], v_b_ref[0], sm_b_ref[0], par_ref[1], c_ref[1], m_ref[1], False)
    yf_ref[0] = y_f
    yb_ref[0] = y_b
    c_ref[...] = jnp.stack([c_f, c_b])
    m_ref[...] = jnp.stack([m_f, m_b])


def _mlstm(conv_out, rest_out, mixpar, n_lat):
    bsz, s_tot, _ = conv_out.shape
    sm_blk = (REST_W - N_SMALL) // LANES
    chunk_specs, nc = _bidir_specs(n_lat, ML_Q, [(BW, 2), (BW, REST_COL['ml_v']), (LANES, lambda d: sm_blk + d)])
    out_specs, _ = _bidir_specs(n_lat, ML_Q, [(BW, 0)])
    return pl.pallas_call(
        _mlstm_kernel,
        grid=(bsz, nc),
        in_specs=chunk_specs + [_const_spec(mixpar.shape)],
        out_specs=out_specs,
        out_shape=[jax.ShapeDtypeStruct((bsz, s_tot, BW), F32)] * 2,
        scratch_shapes=[pltpu.VMEM((2, ML_DK, ML_H * 2 * ML_DV), F32),
                        pltpu.VMEM((2, SUBLANES, LANES), F32)],
        compiler_params=_cparams(2),
        name="mlstm",
    )(conv_out, rest_out, rest_out, conv_out, rest_out, rest_out, mixpar)


def _block_diag(x, n_blocks):
    r, cols = x.shape
    w = cols // n_blocks
    tiled = jnp.concatenate([x] * n_blocks, axis=0)
    assert r & (r - 1) == 0 and w & (w - 1) == 0
    ri = lax.shift_right_logical(lax.broadcasted_iota(jnp.int32, tiled.shape, 0), r.bit_length() - 1)
    ci = lax.shift_right_logical(lax.broadcasted_iota(jnp.int32, tiled.shape, 1), w.bit_length() - 1)
    return jnp.where(ri == ci, tiled, jnp.zeros_like(tiled))


def _split_hi_lo(x):
    hi = x.astype(BF16)
    return hi, (x - hi.astype(F32)).astype(BF16)


def _dot_hi_blocks(lhs, rhs_cat, n_blocks):
    lh, ll = _split_hi_lo(lhs)
    rh, rl = _split_hi_lo(rhs_cat)
    rh, rl = _block_diag(rh, n_blocks), _block_diag(rl, n_blocks)
    return jnp.dot(jnp.concatenate([lh, lh, ll], axis=1), jnp.concatenate([rh, rl, rh], axis=0),
                   preferred_element_type=F32)


def _unit_lower_inverse_blocks(l_cat, n_blocks):
    q = l_cat.shape[0]
    ri = lax.broadcasted_iota(jnp.int32, l_cat.shape, 0)
    ci = lax.broadcasted_iota(jnp.int32, l_cat.shape, 1) & (q - 1)
    p = -l_cat
    t = jnp.where(ri == ci, 1.0, 0.0) + p
    p = _dot_hi_blocks(p, p, n_blocks)
    levels = max(q.bit_length() - 2, 0)
    for k in range(levels):
        if k == levels - 1:
            return t + _dot_hi_blocks(t, p, n_blocks)
        both = _dot_hi_blocks(jnp.concatenate([p, t], axis=0), p, n_blocks)
        p, t = both[:q], t + both[q:]
    return t


def _head_l2norm(x, width, scale):
    parts = []
    for h in range(x.shape[1] // width):
        seg = x[:, h * width:(h + 1) * width]
        parts.append(seg * (lax.rsqrt(jnp.sum(seg * seg, axis=-1, keepdims=True) + EPS) * scale))
    return jnp.concatenate(parts, axis=1)


def _gdn_chunk(qkvs, sms, pars, s_prev, dirs):
    q, nh = GDN_Q, GDN_H
    nb = nh * len(dirs)
    hk = nh * GDN_DK
    cat = lambda parts: jnp.concatenate(parts, axis=1)
    qn = _head_l2norm(cat([x[:, :hk] for x in qkvs]), GDN_DK, GDN_DK ** -0.5)
    kn = _head_l2norm(cat([x[:, hk:2 * hk] for x in qkvs]), GDN_DK, 1.0)
    v = cat([x[:, 2 * hk:] for x in qkvs])
    col = lambda a, lane0, h, w: jnp.broadcast_to(a[:, lane0 + h:lane0 + h + 1], (q, w))
    ri = lax.broadcasted_iota(jnp.int32, (q, nh * q), 0)
    ci = lax.broadcasted_iota(jnp.int32, (q, nh * q), 1) & (q - 1)
    g_c, gx, bx, g_r, glx, ahead = [], [], [], [], [], []
    for sm, par, fwd in zip(sms, pars, dirs):
        lg = -jnp.exp(par[1:2]) * _softplus(sm + par[0:1])
        beta = _sigmoid(sm)
        g = _sel_dot_l(_scan_sel3(q, fwd), lg)
        g_t = g.T
        gx_d = [col(g, L_GA, h, GDN_DK) for h in range(nh)]
        g_c += [col(g, L_GA, h, q) for h in range(nh)]
        gx += gx_d
        bx += [col(beta, L_GB, h, GDN_DK) for h in range(nh)]
        g_r += [g_t[L_GA + h:L_GA + h + 1, :] for h in range(nh)]
        glx += [x[q - 1:q] if fwd else x[0:1] for x in gx_d]
        ahead.append((ri - ci) if fwd else (ci - ri))
    g_c, gx, bx, g_r, glx, ahead = (cat(x) for x in (g_c, gx, bx, g_r, glx, ahead))
    nh = nb
    hk = nb * GDN_DK
    diff = g_c - g_r
    kb = kn * bx
    k_bd = _block_diag(kn.astype(BF16), nh)
    la = _bdot_nt(jnp.concatenate([kb, qn], axis=0), k_bd)
    l_cat = la[:q] * jnp.exp(jnp.where(ahead > 0, diff, -jnp.inf))
    a_qk = la[q:] * jnp.exp(jnp.where(ahead >= 0, diff, -jnp.inf))
    t_inv = _unit_lower_inverse_blocks(l_cat, nh)
    eg = jnp.exp(gx)
    rhs = jnp.concatenate([_block_diag((v * bx).astype(BF16), nh), _block_diag((kb * eg).astype(BF16), nh)], axis=1)
    uw = jnp.dot(t_inv.astype(BF16), rhs, preferred_element_type=F32)
    ws = jnp.dot(jnp.concatenate([uw[:, hk:], qn * eg], axis=0).astype(BF16),
                 _block_diag(s_prev.astype(BF16), nh), preferred_element_type=F32)
    u_bd = _block_diag((uw[:, :hk] - ws[:q]).astype(BF16), nh)
    kd = kn * jnp.exp(glx - gx)
    kd_t = jnp.concatenate([kd[:, h * GDN_DK:(h + 1) * GDN_DK].T for h in range(nh)], axis=1)
    both = jnp.dot(jnp.concatenate([a_qk, kd_t], axis=0).astype(BF16), u_bd, preferred_element_type=F32)
    return ws[q:] + both[:q], jnp.exp(glx) * s_prev + both[q:]


def _gdn_kernel(qkv_f_ref, sm_f_ref, qkv_b_ref, sm_b_ref, par_ref, yf_ref, yb_ref, s_ref):
    @pl.when(pl.program_id(1) == 0)
    def _():
        s_ref[...] = jnp.zeros_like(s_ref)

    y, s_new = _gdn_chunk((qkv_f_ref[0], qkv_b_ref[0]), (sm_f_ref[0], sm_b_ref[0]), (par_ref[0], par_ref[1]),
                          s_ref[...], (True, False))
    yf_ref[0] = y[:, :BW]
    yb_ref[0] = y[:, BW:]
    s_ref[...] = s_new


def _bidir_specs(n_lat, q, blocks):
    mem_chunk, nc = _chunk_index(n_lat, q)
    specs = []
    for d in range(2):
        for width, col in blocks:
            c = col(d) if callable(col) else col
            specs.append(pl.BlockSpec((1, q, width), lambda b, j, d=d, c=c: (b, mem_chunk(d, j), c)))
    return specs, nc


def _gdn(conv_out, rest_out, mixpar, n_lat):
    bsz, s_tot, _ = conv_out.shape
    sm_blk = (REST_W - N_SMALL) // LANES
    chunk_specs, nc = _bidir_specs(n_lat, GDN_Q, [(3 * BW, 1), (LANES, lambda d: sm_blk + d)])
    out_specs, _ = _bidir_specs(n_lat, GDN_Q, [(BW, 0)])
    return pl.pallas_call(
        _gdn_kernel,
        grid=(bsz, nc),
        in_specs=chunk_specs + [_const_spec(mixpar.shape)],
        out_specs=out_specs,
        out_shape=[jax.ShapeDtypeStruct((bsz, s_tot, BW), F32)] * 2,
        scratch_shapes=[pltpu.VMEM((GDN_DK, 2 * GDN_H * GDN_DV), F32)],
        compiler_params=_cparams(2),
        name="gdn",
    )(conv_out, rest_out, conv_out, rest_out, mixpar)


def _s5_scan(sloc, h_init, pw, fwd, n_valid):
    n = sloc.shape[0]
    row = lax.broadcasted_iota(jnp.int32, (n, 1), 0)

    def mul(x, i):
        return pw[2 * i:2 * i + 1] * x + pw[2 * i + 1:2 * i + 2] * pltpu.roll(x, S5_P, axis=1)

    pos = jnp.where(fwd, row, n - 1 - row)
    first = jnp.where(fwd, 0, n_valid - 1)
    z = pltpu.roll(sloc, jnp.where(fwd, 1, n - 1), axis=0)
    z = jnp.where(pos >= jnp.where(fwd, 1, n - n_valid + 1), z, 0.0)
    z = jnp.where(row == first, h_init, z)
    s, i = 1, 0
    while s < n:
        zs = pltpu.roll(z, jnp.where(fwd, s, n - s), axis=0)
        zs = jnp.where(pos >= s, zs, 0.0)
        z = z + mul(zs, i)
        s, i = 2 * s, i + 1
    last = jnp.where(fwd, n_valid - 1, 0)
    pick = row == last
    z_last = jnp.sum(jnp.where(pick, z, 0.0), axis=0, keepdims=True)
    s_last = jnp.sum(jnp.where(pick, sloc, 0.0), axis=0, keepdims=True)
    return z, mul(z_last, 0) + s_last


def _s5_segment(load_rows, store_rows, n_rows, n_valid, fwd, ut_ref, st_ref, wcat_ref, pm_ref, pw_ref,
                skip_row):
    qg = S5_Q * S5_GC
    for t in range(S5_Q):
        vt = load_rows(t)
        ut_ref[:, t * S5_GC:(t + 1) * S5_GC, :] = vt.T.reshape(S5_G, S5_GC, n_rows)

    def group(gi, carry):
        ug = ut_ref[gi].T[:n_valid].astype(BF16)
        y1 = jnp.dot(ug, wcat_ref[0, gi], preferred_element_type=F32)
        h_start, h_end = _s5_scan(y1[:, qg:], st_ref[gi, 0:1, :], pw_ref[0, gi], fwd, n_valid)
        st_ref[gi, 0:1, :] = h_end
        yy = y1[:, :qg] + jnp.dot(h_start.astype(BF16), pm_ref[0, gi], preferred_element_type=F32)
        if n_valid < n_rows:
            yy = jnp.concatenate([yy, jnp.zeros((n_rows - n_valid, qg), F32)], axis=0)
        ut_ref[gi] = yy.T
        return carry

    lax.fori_loop(0, S5_G, group, 0, unroll=S5_UNROLL)
    for t in range(S5_Q):
        wt = ut_ref[:, t * S5_GC:(t + 1) * S5_GC, :].reshape(BW, n_rows).T
        store_rows(t, wt + skip_row * load_rows(t))


def _load_token(u_ref, t, n_chunks):
    return jnp.concatenate([u_ref[0, k, pl.ds(t, n_chunks, stride=S5_Q), :] for k in range(S5_SPLIT)], axis=1)


def _store_token(y_ref, t, n_chunks, val):
    for k in range(S5_SPLIT):
        y_ref[0, 0, k, pl.ds(t, n_chunks, stride=S5_Q), :] = val[:, k * LANES:(k + 1) * LANES]


def _s5_kernel(ul_ref, uc_ref, wcat_ref, pm_ref, pw_ref, dsk_ref, yl_ref, yc_ref, ut_ref, utc_ref, st_ref,
               *, n_seg_chunks):
    d = pl.program_id(1)
    fwd = d == 0
    skip_row = jnp.where(fwd, dsk_ref[...], 0.0)
    ncc = TM // S5_Q

    @pl.when(pl.program_id(2) == 0)
    def _():
        st_ref[...] = jnp.zeros_like(st_ref)
        pad = jnp.zeros((LANES - ncc, BW), F32)

        def load_c(t):
            return jnp.concatenate([_load_token(uc_ref, t, ncc), pad], axis=0)

        def store_c(t, val):
            _store_token(yc_ref, t, ncc, val[:ncc])

        _s5_segment(load_c, store_c, LANES, ncc, fwd, utc_ref, st_ref, wcat_ref, pm_ref, pw_ref, skip_row)

    def load_l(t):
        return _load_token(ul_ref, t, n_seg_chunks)

    def store_l(t, val):
        _store_token(yl_ref, t, n_seg_chunks, val)

    _s5_segment(load_l, store_l, n_seg_chunks, n_seg_chunks, fwd, ut_ref, st_ref, wcat_ref, pm_ref, pw_ref,
                skip_row)


def _s5(u_slabs, wcat, pm, pw, dskip, n_lat):
    bsz = u_slabs.shape[0]
    seg = min(S5_SEG, n_lat)
    nseg = n_lat // seg
    nsc = seg // S5_Q
    qg = S5_Q * S5_GC
    seg_mem = lambda d, s: jnp.where(d == 0, s, nseg - 1 - s)
    ctx_blk = n_lat // TM
    yl, yc = pl.pallas_call(
        functools.partial(_s5_kernel, n_seg_chunks=nsc),
        grid=(bsz, 2, nseg),
        in_specs=[pl.BlockSpec((1, S5_SPLIT, seg, LANES), lambda b, d, s: (b, 0, seg_mem(d, s), 0)),
                  pl.BlockSpec((1, S5_SPLIT, TM, LANES), lambda b, d, s: (b, 0, ctx_blk, 0)),
                  pl.BlockSpec((1, S5_G, qg, qg + 2 * S5_P), lambda b, d, s: (d, 0, 0, 0)),
                  pl.BlockSpec((1, S5_G, 2 * S5_P, qg), lambda b, d, s: (d, 0, 0, 0)),
                  pl.BlockSpec((1, S5_G, 2 * SUBLANES, LANES), lambda b, d, s: (d, 0, 0, 0)),
                  _const_spec((1, BW))],
        out_specs=[pl.BlockSpec((1, 1, S5_SPLIT, seg, LANES), lambda b, d, s: (d, b, 0, seg_mem(d, s), 0)),
                   pl.BlockSpec((1, 1, S5_SPLIT, TM, LANES), lambda b, d, s: (d, b, 0, 0, 0))],
        out_shape=[jax.ShapeDtypeStruct((2, bsz, S5_SPLIT, n_lat, LANES), F32),
                   jax.ShapeDtypeStruct((2, bsz, S5_SPLIT, TM, LANES), F32)],
        scratch_shapes=[pltpu.VMEM((S5_G, qg, nsc), F32),
                        pltpu.VMEM((S5_G, qg, LANES), F32),
                        pltpu.VMEM((S5_G, SUBLANES, LANES), F32)],
        compiler_params=_cparams(3),
        name="s5",
    )(u_slabs, u_slabs, wcat, pm, pw, dskip)
    return yl, yc


def _s5_operators(a_re, a_im, log_dt, b_re, b_im, c_re, c_im):
    hi = lax.Precision.HIGHEST
    q, nq = S5_Q, S5_Q + 1
    lre = jnp.minimum(a_re.astype(F32), -1e-4)
    lim = a_im.astype(F32)
    dt = jnp.exp(log_dt.astype(F32))[..., None]
    mag = jnp.exp(lre * dt)
    ab_re, ab_im = mag * jnp.cos(lim * dt), mag * jnp.sin(lim * dt)
    den = lre * lre + lim * lim
    f_re = ((ab_re - 1.0) * lre + ab_im * lim) / den
    f_im = (ab_im * lre - (ab_re - 1.0) * lim) / den
    br, bi = b_re.astype(F32), b_im.astype(F32)
    bb_re = f_re[..., None] * br - f_im[..., None] * bi
    bb_im = f_re[..., None] * bi + f_im[..., None] * br
    pr, pi = [jnp.ones_like(ab_re)], [jnp.zeros_like(ab_im)]
    for _ in range(q):
        pr, pi = pr + [pr[-1] * ab_re - pi[-1] * ab_im], pi + [pr[-1] * ab_im + pi[-1] * ab_re]
    pw_re, pw_im = jnp.stack(pr, axis=2), jnp.stack(pi, axis=2)
    e_re = pw_re[..., None] * bb_re[:, :, None] - pw_im[..., None] * bb_im[:, :, None]
    e_im = pw_re[..., None] * bb_im[:, :, None] + pw_im[..., None] * bb_re[:, :, None]
    cr, ci = c_re.astype(F32), c_im.astype(F32)
    kern = (jnp.einsum('gcp,dgnpe->dgnce', cr, e_re, precision=hi)
            - jnp.einsum('gcp,dgnpe->dgnce', ci, e_im, precision=hi))
    s_idx = jnp.arange(q)[:, None]
    t_idx = jnp.arange(q)[None, :]
    mats, rmats, pmats = [], [], []
    for d in range(2):
        lag = (t_idx - s_idx) if d == 0 else (s_idx - t_idx)
        valid = lag >= 0
        m = kern[d][:, jnp.clip(lag, 0, q)]
        m = jnp.where(valid[None, :, :, None, None], m, 0.0)
        mats.append(jnp.transpose(m, (0, 1, 4, 2, 3)).reshape(S5_G, q * S5_GC, q * S5_GC))
        e_s = (q - 1 - jnp.arange(q)) if d == 0 else jnp.arange(q)
        r = jnp.concatenate([e_re[d][:, e_s], e_im[d][:, e_s]], axis=2)
        rmats.append(jnp.transpose(r, (0, 1, 3, 2)).reshape(S5_G, q * S5_GC, 2 * S5_P))
        f_t = (jnp.arange(q) + 1) if d == 0 else (q - jnp.arange(q))
        fr, fi = pw_re[d][:, f_t], pw_im[d][:, f_t]
        p_from_re = cr[:, None] * fr[:, :, None] - ci[:, None] * fi[:, :, None]
        p_from_im = -cr[:, None] * fi[:, :, None] - ci[:, None] * fr[:, :, None]
        p = jnp.concatenate([p_from_re, p_from_im], axis=3)
        pmats.append(jnp.transpose(p, (0, 3, 1, 2)).reshape(S5_G, 2 * S5_P, q * S5_GC))
    wcat = jnp.concatenate([jnp.stack(mats), jnp.stack(rmats)], axis=3).astype(BF16)
    pm = jnp.stack(pmats).astype(BF16)
    dr, di = pw_re[:, :, q], pw_im[:, :, q]
    rows = []
    for _ in range(SUBLANES):
        rows += [jnp.concatenate([dr, dr], axis=-1), jnp.concatenate([-di, di], axis=-1)]
        dr, di = dr * dr - di * di, 2.0 * dr * di
    return wcat, pm, jnp.stack(rows, axis=2)


def _head_rms(x, w, width):
    parts = []
    for h in range(x.shape[1] // width):
        seg = x[:, h * width:(h + 1) * width]
        parts.append(seg * lax.rsqrt(jnp.mean(seg * seg, axis=-1, keepdims=True) + EPS))
    return jnp.concatenate(parts, axis=1) * w


def _merge_kernel(h_ref, mod_ref, nw_ref, wg_ref, gb_ref, yaf_ref, yab_ref, ybf_ref, ybb_ref, yc_ref, ydf_ref, ydb_ref,
                  za_ref, ob_ref,
                  zd_ref, na_ref, nb_ref, nd_ref, glw_ref, glb_ref, wbr_ref, wo_ref, o_ref, *, col_rows):
    x = _tile_rows(h_ref[0], col_rows)
    mod = mod_ref[0]
    xb = (_rms(x, nw_ref[...]) * (1.0 + mod[4:5]) + mod[3:4]).astype(BF16)
    gates = _sigmoid(jnp.dot(xb, wg_ref[...], preferred_element_type=F32) + gb_ref[...])
    ya = _rms((yaf_ref[0] + yab_ref[0]) * _silu(za_ref[0]), na_ref[...])
    yb = _sigmoid(ob_ref[0]) * _head_rms(ybf_ref[0] + ybb_ref[0], nb_ref[...], ML_DV)
    gc = _gelu_tanh(jnp.concatenate([yc_ref[0, 0, k] + yc_ref[1, 0, k] for k in range(S5_SPLIT)], axis=1))
    yc = gc * _sigmoid(_bdot(gc, glw_ref[...]) + glb_ref[...])
    yd = _head_rms(ydf_ref[0] + ydb_ref[0], nd_ref[...], GDN_DV) * _silu(zd_ref[0])
    merged = jnp.zeros((TM, D), F32)
    for k, yk in enumerate((ya, yb, yc, yd)):
        merged = merged + gates[:, k * D:(k + 1) * D] * _bdot(yk, wbr_ref[k])
    out = _bdot(merged, wo_ref[...])
    o_ref[0] = _tile_unrows(x + mod[5:6] * out, col_rows)


def _merge(h, mod, mod_row, tile0, nw, wg, gb, ys, rest_out, norms, glw, glb, wbr, wo, col_rows):
    bsz, n, _ = h.shape
    nt = n // TM
    tile_shape, view = _lat_tile_spec(n, col_rows)
    tile_map = (lambda b, t: (b, t, 0)) if col_rows is None else (lambda b, t: (b, 0, t))
    row = (lambda b: b) if mod_row is None else (lambda b: mod_row)
    s5_spec = lambda t0: pl.BlockSpec((2, 1, S5_SPLIT, TM, LANES), lambda b, t: (0, b, 0, t0 + t, 0))
    one_spec = lambda t0: pl.BlockSpec((1, TM, BW), lambda b, t: (b, t0 + t, 0))
    specs = (one_spec, one_spec, one_spec, one_spec, s5_spec, one_spec, one_spec)
    r_spec = lambda c: pl.BlockSpec((1, TM, BW), lambda b, t: (b, tile0 + t, c))
    hv = view(h)
    out = pl.pallas_call(
        functools.partial(_merge_kernel, col_rows=col_rows),
        grid=(bsz, nt),
        in_specs=[pl.BlockSpec(tile_shape, tile_map),
                  pl.BlockSpec((1, N_MOD, D), lambda b, t: (row(b), 0, 0)),
                  _const_spec((1, D)),
                  _const_spec((D, N_BRANCH * D)),
                  _const_spec((1, N_BRANCH * D)),
                  *[mk(t0) for mk, (_, t0) in zip(specs, ys)],
                  r_spec(REST_COL['ssm_z']), r_spec(REST_COL['ml_o']), r_spec(REST_COL['gdn_z']),
                  _const_spec((1, BW)), _const_spec((1, BW)), _const_spec((1, BW)),
                  _const_spec((BW, BW)), _const_spec((1, BW)),
                  _const_spec((N_BRANCH, BW, D)),
                  _const_spec((D, D))],
        out_specs=pl.BlockSpec(tile_shape, tile_map),
        out_shape=jax.ShapeDtypeStruct(hv.shape, F32),
        compiler_params=_cparams(2),
        name="merge",
    )(hv, mod, nw.reshape(1, D), wg, gb, *[a for a, _ in ys], rest_out, rest_out, rest_out, *norms, glw, glb, wbr, wo)
    return out.reshape(h.shape)


def _layer_params(l, p):
    offs = [0]
    for w in IN_SPLITS:
        offs.append(offs[-1] + w)
    w_in = p['w_in'][l]
    col = lambda i: w_in[:, offs[i]:offs[i + 1]]
    z128 = lambda k: jnp.zeros((D, k), F32)

    def small(d):
        dt = col(2).reshape(D, 2, SSM_H)[:, d]
        gates = col(6).reshape(D, 2, 2 * ML_H)[:, d]
        ga = col(10).reshape(D, 2, GDN_H)[:, d]
        gb = col(11).reshape(D, 2, GDN_H)[:, d]
        used = SSM_H + 2 * ML_H + 2 * GDN_H
        return jnp.concatenate([dt, gates, ga, gb, z128(LANES - used)], axis=1)

    wc = jnp.concatenate([col(1), col(3), col(8)], axis=1).astype(BF16)
    wr = jnp.concatenate([col(0), col(4), col(5), col(9), small(0), small(1), col(7)], axis=1).astype(BF16)
    cw = jnp.concatenate([p['ssm_conv_w'][l], p['ml_conv_w'][l], p['gdn_conv_w'][l]], axis=1)
    cw = jnp.concatenate([cw, jnp.zeros((SUBLANES - CONV_K, CONV_W), F32)], axis=0)
    cb = jnp.concatenate([p['ssm_conv_b'][l], p['ml_conv_b'][l], jnp.zeros((GDN_H * (2 * GDN_DK + GDN_DV),), F32)])

    def par_rows(d):
        pad = jnp.zeros((LANES - L_GB,), F32)
        bias = jnp.concatenate([p['ssm_dt_bias'][l, d], p['ml_gate_b'][l, d].reshape(-1),
                                p['gdn_dt_bias'][l, d], pad])
        alog = jnp.concatenate([p['ssm_a_log'][l, d], jnp.zeros((2 * ML_H,), F32), p['gdn_a_log'][l, d], pad])
        return jnp.concatenate([bias[None], alog[None], jnp.zeros((SUBLANES - 2, LANES), F32)], axis=0)

    mixpar = jnp.stack([par_rows(0), par_rows(1)])

    def parx_rows(d):
        rows = jnp.stack([jnp.repeat(p['ssm_a_log'][l, d], SSM_P), jnp.repeat(p['ssm_d'][l], SSM_P)])
        return jnp.concatenate([rows, jnp.zeros((SUBLANES - 2, BW), F32)], axis=0)

    ssm_parx = jnp.stack([parx_rows(0), parx_rows(1)])
    wcat, pm, pw = _s5_operators(p['s5_a_re'][l], p['s5_a_im'][l], p['s5_log_dt'][l], p['s5_b_re'][l],
                                 p['s5_b_im'][l], p['s5_c_re'][l], p['s5_c_im'][l])
    return dict(
        wc=wc, wr=wr, cw=cw, cb=cb.reshape(1, CONV_W), mixpar=mixpar, ssm_parx=ssm_parx,
        s5_wcat=wcat, s5_pm=pm, s5_pw=pw, s5_d=p['s5_d'][l].reshape(1, BW),
        wg=col(12).astype(BF16), gb=p['gate_b'][l].reshape(1, N_BRANCH * D),
        norms=(p['ssm_norm'][l].reshape(1, BW), p['ml_norm'][l].reshape(1, BW),
               jnp.tile(p['gdn_norm'][l], GDN_H).reshape(1, BW)),
        glw=p['s5_glu_w'][l].astype(BF16), glb=p['s5_glu_b'][l].reshape(1, BW),
        wbr=p['w_branch'][l].astype(BF16), wo=p['w_out'][l].astype(BF16),
        ffn_up=p['ffn_up'][l].astype(BF16), ffn_dn=p['ffn_down'][l].astype(BF16))


def kernel(x, c, ctx, c_ctx, ada_w, ada_b, norm_w, ffn_up, ffn_down, w_in, ssm_conv_w, ssm_conv_b, ssm_dt_bias, ssm_a_log, ssm_d, ssm_norm, ml_conv_w, ml_conv_b, ml_gate_b, ml_norm, s5_a_re, s5_a_im, s5_log_dt, s5_b_re, s5_b_im, s5_c_re, s5_c_im, s5_d, s5_glu_w, s5_glu_b, gdn_conv_w, gdn_dt_bias, gdn_a_log, gdn_norm, gate_b, w_branch, w_out, final_norm):
    p = dict(ffn_up=ffn_up, ffn_down=ffn_down, w_in=w_in, ssm_conv_w=ssm_conv_w, ssm_conv_b=ssm_conv_b,
             ssm_dt_bias=ssm_dt_bias, ssm_a_log=ssm_a_log, ssm_d=ssm_d, ssm_norm=ssm_norm, ml_conv_w=ml_conv_w,
             ml_conv_b=ml_conv_b, ml_gate_b=ml_gate_b, ml_norm=ml_norm, s5_a_re=s5_a_re, s5_a_im=s5_a_im,
             s5_log_dt=s5_log_dt, s5_b_re=s5_b_re, s5_b_im=s5_b_im, s5_c_re=s5_c_re, s5_c_im=s5_c_im, s5_d=s5_d,
             s5_glu_w=s5_glu_w, s5_glu_b=s5_glu_b, gdn_conv_w=gdn_conv_w, gdn_dt_bias=gdn_dt_bias,
             gdn_a_log=gdn_a_log, gdn_norm=gdn_norm, gate_b=gate_b, w_branch=w_branch, w_out=w_out)
    bsz, n, _ = x.shape
    depth = ada_w.shape[0]
    rows = n // GRID_W
    assert ctx.shape[1] == TM and n % TM == 0 and bsz + 1 <= SUBLANES
    assert TM % rows == 0 and rows % HALO == 0 and n % min(S5_SEG, n) == 0
    cvec = jnp.concatenate([c, c_ctx[None], jnp.zeros((SUBLANES - bsz - 1, D), F32)], axis=0)
    modtab = _mod_table(cvec, ada_w, ada_b)
    lane_head = jnp.arange(LANES)[:, None] == (jnp.arange(BW)[None, :] // SSM_P)
    expand = lane_head.astype(BF16)
    h_lat, h_ctx = x, ctx
    nt = n // TM
    for l in range(depth):
        lp = _layer_params(l, p)
        mod = modtab[l]
        need_ctx = l < depth - 1
        col_rows = rows if l % 2 == 1 else None
        h_lat = _ffn(h_lat, mod, None, norm_w[l, 0], lp['ffn_up'][0], lp['ffn_dn'][0], 0)
        h_ctx = _ffn(h_ctx, mod, bsz, norm_w[l, 0], lp['ffn_up'][0], lp['ffn_dn'][0], 0)
        conv_out, rest_out, u_s5 = _inproj(h_lat, h_ctx, mod, norm_w[l, 1], lp['wc'], lp['wr'], lp['cw'],
                                           lp['cb'], col_rows)
        y_ssd = _ssd(conv_out, rest_out, lp['mixpar'], lp['ssm_parx'], expand, n)
        y_ml = _mlstm(conv_out, rest_out, lp['mixpar'], n)
        y_s5_lat, y_s5_ctx = _s5(u_s5, lp['s5_wcat'], lp['s5_pm'], lp['s5_pw'], lp['s5_d'], n)
        y_gdn = _gdn(conv_out, rest_out, lp['mixpar'], n)
        tail = (rest_out, lp['norms'], lp['glw'], lp['glb'], lp['wbr'], lp['wo'])
        head = (norm_w[l, 1], lp['wg'], lp['gb'])
        ys_lat = ((y_ssd[0], 0), (y_ssd[1], 0), (y_ml[0], 0), (y_ml[1], 0), (y_s5_lat, 0), (y_gdn[0], 0), (y_gdn[1], 0))
        ys_ctx = ((y_ssd[0], nt), (y_ssd[1], nt), (y_ml[0], nt), (y_ml[1], nt), (y_s5_ctx, 0), (y_gdn[0], nt), (y_gdn[1], nt))
        h_lat = _merge(h_lat, mod, None, 0, *head, ys_lat, *tail, col_rows)
        last = l == depth - 1
        h_lat = _ffn(h_lat, mod, None, norm_w[l, 2], lp['ffn_up'][1], lp['ffn_dn'][1], 6,
                     final_w=final_norm if last else None)
        if need_ctx:
            h_ctx = _merge(h_ctx, mod, bsz, nt, *head, ys_ctx, *tail, None)
            h_ctx = _ffn(h_ctx, mod, bsz, norm_w[l, 2], lp['ffn_up'][1], lp['ffn_dn'][1], 6)
    return h_lat
```

```python
import functools

import jax
import jax.numpy as jnp
from jax import lax
from jax.experimental import pallas as pl
from jax.experimental.pallas import tpu as pltpu

F32 = jnp.float32
BF16 = jnp.bfloat16

D = 1024
GRID_W = 64
N_MOD = 9
D_FF = 2816
CONV_K = 5
EPS = 1e-6
N_BRANCH = 4
BW = 512
SSM_H, SSM_P, SSM_G, SSM_N, SSM_Q = 8, 64, 2, 128, 128
ML_H, ML_DK, ML_DV, ML_Q = 4, 64, 128, 128
S5_G, S5_GC, S5_P, S5_Q = 32, 16, 64, 16
GDN_H, GDN_DK, GDN_DV, GDN_Q = 4, 128, 128, 64
IN_SPLITS = (512, 1024, 16, 512, 512, 512, 16, 512, 1536, 512, 8, 8, 4096)

LANES = 128
SUBLANES = 8
VMEM_LIMIT = 56 * 1024 * 1024

TM = 256
FFN_TM = 512
HALO = SUBLANES
CONV_W = 1024 + 512 + 1536
N_SMALL = 2 * LANES
REST_W = 4 * BW + N_SMALL
REST_COL = dict(ssm_z=0, ml_v=1, ml_o=2, gdn_z=3)
S5_SPLIT = BW // LANES
L_DT, L_IG, L_FG, L_GA, L_GB = 0, 8, 12, 16, 20
S5_SEG = 2048
S5_UNROLL = 4


def _cparams(n_axes):
    return pltpu.CompilerParams(dimension_semantics=("arbitrary",) * n_axes,
                                vmem_limit_bytes=VMEM_LIMIT)


def _const_spec(shape, single_buffer=False):
    zeros = (0,) * len(shape)
    if single_buffer:
        return pl.BlockSpec(shape, lambda *_: zeros, pipeline_mode=pl.Buffered(1))
    return pl.BlockSpec(shape, lambda *_: zeros)


def _sigmoid(x):
    return 1.0 / (1.0 + jnp.exp(-x))


def _silu(x):
    return x * _sigmoid(x)


def _softplus(x):
    return jnp.maximum(x, 0.0) + jnp.log1p(jnp.exp(-jnp.abs(x)))


def _gelu_tanh(x):
    return 0.5 * x * (1.0 + jnp.tanh(0.7978845608028654 * (x + 0.044715 * (x * x * x))))


def _rms(x, w):
    return x * lax.rsqrt(jnp.mean(x * x, axis=-1, keepdims=True) + EPS) * w


def _bdot(a, b):
    return jnp.dot(a.astype(BF16), b.astype(BF16), preferred_element_type=F32)


def _bdot_nt(a, b):
    return lax.dot_general(a.astype(BF16), b.astype(BF16), (((1,), (1,)), ((), ())),
                           preferred_element_type=F32)


def _bdot_tn(a, b):
    return lax.dot_general(a.astype(BF16), b.astype(BF16), (((0,), (0,)), ((), ())),
                           preferred_element_type=F32)


def _split3(x):
    hi = x.astype(BF16)
    r = x - hi.astype(F32)
    mid = r.astype(BF16)
    lo = (r - mid.astype(F32)).astype(BF16)
    return hi, mid, lo


def _scan_sel3(q, fwd, all_ones=False):
    ri = lax.broadcasted_iota(jnp.int32, (q, 3 * LANES), 0)
    cj = lax.broadcasted_iota(jnp.int32, (q, 3 * LANES), 1) & (LANES - 1)
    ahead = jnp.zeros_like(ri) if all_ones else (ri - cj) * jnp.where(fwd, 1, -1)
    return jnp.where(jnp.where(cj < q, ahead, -1) >= 0, 1.0, 0.0).astype(BF16)


def _sel_dot_l(sel3, x):
    rows = []
    for p in _split3(x):
        rows.append(p)
        if p.shape[0] < LANES:
            rows.append(jnp.zeros((LANES - p.shape[0], p.shape[1]), BF16))
    return jnp.dot(sel3, jnp.concatenate(rows, axis=0), preferred_element_type=F32)


def _sel_dot_r(x, sel):
    return jnp.dot(jnp.concatenate(_split3(x), axis=1), jnp.concatenate([sel] * 3, axis=0),
                   preferred_element_type=F32)


def _scan_masks(q, fwd):
    ri = lax.broadcasted_iota(jnp.int32, (q, q), 0)
    ci = lax.broadcasted_iota(jnp.int32, (q, q), 1)
    ahead = (ri - ci) * jnp.where(fwd, 1, -1)
    return ahead >= 0, ahead > 0


def _mod_kernel(c_ref, w_ref, b_ref, o_ref):
    s = _silu(c_ref[...])
    o_ref[0, 0] = _bdot(s, w_ref[0]) + b_ref[0, 0]


def _mod_table(cvec, ada_w, ada_b):
    depth = ada_w.shape[0]
    out = pl.pallas_call(
        _mod_kernel,
        grid=(depth, N_MOD),
        in_specs=[_const_spec((SUBLANES, D)),
                  pl.BlockSpec((1, D, D), lambda l, j: (l, 0, j)),
                  pl.BlockSpec((1, 1, 1, D), lambda l, j: (l, j, 0, 0))],
        out_specs=pl.BlockSpec((1, 1, SUBLANES, D), lambda l, j: (l, j, 0, 0)),
        out_shape=jax.ShapeDtypeStruct((depth, N_MOD, SUBLANES, D), F32),
        compiler_params=_cparams(2),
        name="mod_table",
    )(cvec, ada_w, ada_b.reshape(depth, N_MOD, 1, D))
    return jnp.transpose(out, (0, 2, 1, 3))


def _ffn_kernel(h_ref, mod_ref, nw_ref, wup_ref, wdn_ref, *rest, base, final):
    o_ref = rest[-1]
    x = h_ref[0]
    mod = mod_ref[0]
    xm = _rms(x, nw_ref[...]) * (1.0 + mod[base + 1:base + 2]) + mod[base:base + 1]
    xb = xm.astype(BF16)
    g = jnp.dot(xb, wup_ref[:, :D_FF], preferred_element_type=F32)
    u = jnp.dot(xb, wup_ref[:, D_FF:], preferred_element_type=F32)
    y = jnp.dot((_silu(g) * u).astype(BF16), wdn_ref[...], preferred_element_type=F32)
    out = x + (0.5 * mod[base + 2:base + 3]) * y
    if final:
        out = _rms(out, rest[0][...])
    o_ref[0] = out


def _ffn(h, mod, mod_row, nw, w_up, w_dn, base, final_w=None):
    bsz, n, _ = h.shape
    row = (lambda b: b) if mod_row is None else (lambda b: mod_row)
    tm = FFN_TM if n % FFN_TM == 0 else TM
    in_specs = [pl.BlockSpec((1, tm, D), lambda b, t: (b, t, 0)),
                pl.BlockSpec((1, N_MOD, D), lambda b, t: (row(b), 0, 0)),
                _const_spec((1, D)),
                _const_spec((D, 2 * D_FF), single_buffer=True),
                _const_spec((D_FF, D), single_buffer=True)]
    args = [h, mod, nw.reshape(1, D), w_up, w_dn]
    if final_w is not None:
        in_specs.append(_const_spec((1, D)))
        args.append(final_w.reshape(1, D))
    return pl.pallas_call(
        functools.partial(_ffn_kernel, base=base, final=final_w is not None),
        grid=(bsz, n // tm),
        in_specs=in_specs,
        out_specs=pl.BlockSpec((1, tm, D), lambda b, t: (b, t, 0)),
        out_shape=jax.ShapeDtypeStruct(h.shape, F32),
        compiler_params=_cparams(2),
        name="ffn",
    )(*args)


def _lat_tile_spec(n, col_rows):
    if col_rows is None:
        return (1, TM, D), (lambda a: a)
    wpt = TM // col_rows
    return (1, col_rows, wpt * D), (lambda a: a.reshape(a.shape[0], col_rows, GRID_W * D))


def _tile_rows(blk, col_rows):
    if col_rows is None:
        return blk
    wpt = TM // col_rows
    return jnp.concatenate([blk[:, k * D:(k + 1) * D] for k in range(wpt)], axis=0)


def _tile_unrows(x, col_rows):
    if col_rows is None:
        return x
    wpt = TM // col_rows
    return jnp.concatenate([x[k * col_rows:(k + 1) * col_rows] for k in range(wpt)], axis=1)


def _inproj_kernel(mod_ref, hl_ref, hp_ref, hn_ref, hc_ref, nw_ref, wc_ref, wr_ref, cw_ref, cb_ref,
                   oc_ref, or_ref, ou_ref, *, nt, col_rows):
    t = pl.program_id(1)
    is_ctx = t == nt
    mod = mod_ref[0]

    def modulated(x):
        return _rms(x, nw_ref[...]) * (1.0 + mod[4:5]) + mod[3:4]

    has_prev = jnp.logical_and(jnp.logical_not(is_ctx), t >= 1)
    has_next = jnp.logical_and(jnp.logical_not(is_ctx), t <= nt - 2)
    x_prev = modulated(hp_ref[0])
    x_next = modulated(hn_ref[0])
    x_prev = jnp.where(has_prev, x_prev, jnp.zeros_like(x_prev))
    x_next = jnp.where(has_next, x_next, jnp.zeros_like(x_next))
    x_main = modulated(jnp.where(is_ctx, hc_ref[0], _tile_rows(hl_ref[0], col_rows)))
    xb = jnp.concatenate([x_prev, x_main, x_next], axis=0).astype(BF16)
    pc = jnp.dot(xb, wc_ref[...], preferred_element_type=F32)
    acc = cb_ref[...]
    for k in range(CONV_K):
        off = HALO - CONV_K // 2 + k
        acc = acc + cw_ref[k:k + 1, :] * pc[off:off + TM]
    oc_ref[0] = _silu(acc)
    pr = jnp.dot(x_main.astype(BF16), wr_ref[...], preferred_element_type=F32)
    or_ref[0] = pr[:, :REST_W]
    for k in range(S5_SPLIT):
        ou_ref[0, k] = pr[:, REST_W + k * LANES:REST_W + (k + 1) * LANES]


def _inproj(h_lat, h_ctx, mod, nw, wc, wr, cw, cb, col_rows):
    bsz, n, _ = h_lat.shape
    nt = n // TM
    s_tot = n + TM
    tile_shape, view = _lat_tile_spec(n, col_rows)
    lat_t = lambda t: jnp.minimum(t, nt - 1)
    if col_rows is None:
        hb = TM // HALO
        nb = n // HALO
        tile_map = lambda b, t: (b, lat_t(t), 0)
        prev_map = lambda b, t: (b, jnp.clip(t * hb - 1, 0, nb - 1), 0)
        next_map = lambda b, t: (b, jnp.clip((t + 1) * hb, 0, nb - 1), 0)
    else:
        wpt = TM // col_rows
        tile_map = lambda b, t: (b, 0, lat_t(t))
        prev_map = lambda b, t: (b, col_rows // HALO - 1, jnp.clip(t * wpt - 1, 0, GRID_W - 1))
        next_map = lambda b, t: (b, 0, jnp.clip((t + 1) * wpt, 0, GRID_W - 1))
    hv = view(h_lat)
    return pl.pallas_call(
        functools.partial(_inproj_kernel, nt=nt, col_rows=col_rows),
        grid=(bsz, nt + 1),
        in_specs=[pl.BlockSpec((1, N_MOD, D), lambda b, t: (jnp.where(t == nt, bsz, b), 0, 0)),
                  pl.BlockSpec(tile_shape, tile_map),
                  pl.BlockSpec((1, HALO, D), prev_map),
                  pl.BlockSpec((1, HALO, D), next_map),
                  pl.BlockSpec((1, TM, D), lambda b, t: (b, 0, 0)),
                  _const_spec((1, D)),
                  _const_spec((D, CONV_W)),
                  _const_spec((D, REST_W + BW)),
                  _const_spec((SUBLANES, CONV_W)),
                  _const_spec((1, CONV_W))],
        out_specs=[pl.BlockSpec((1, TM, CONV_W), lambda b, t: (b, t, 0)),
                   pl.BlockSpec((1, TM, REST_W), lambda b, t: (b, t, 0)),
                   pl.BlockSpec((1, S5_SPLIT, TM, LANES), lambda b, t: (b, 0, t, 0))],
        out_shape=[jax.ShapeDtypeStruct((bsz, s_tot, CONV_W), F32),
                   jax.ShapeDtypeStruct((bsz, s_tot, REST_W), F32),
                   jax.ShapeDtypeStruct((bsz, S5_SPLIT, s_tot, LANES), F32)],
        compiler_params=_cparams(2),
        name="inproj",
    )(mod, hv, hv, hv, h_ctx, nw.reshape(1, D), wc, wr, cw, cb)


def _chunk_index(n_lat, q):
    ncl, ncc = n_lat // q, TM // q

    def mem_chunk(d, j):
        ctx = jnp.where(d == 0, ncl + j, ncl + ncc - 1 - j)
        lat = jnp.where(d == 0, j - ncc, ncl - 1 - (j - ncc))
        return jnp.where(j < ncc, ctx, lat)
    return mem_chunk, ncl + ncc


def _ssd_chunk(xbc, sm, par, parx, ex, h_prev, fwd):
    q = SSM_Q
    xs = xbc[:, :BW]
    bm = xbc[:, BW:BW + SSM_G * SSM_N].astype(BF16)
    cm = xbc[:, BW + SSM_G * SSM_N:].astype(BF16)
    lane = lax.broadcasted_iota(jnp.int32, (1, LANES), 1)
    dt = jnp.where(lane < SSM_H, _softplus(sm + par[0:1]), 0.0)
    da = dt * (-jnp.exp(par[1:2]))
    incl, _ = _scan_masks(q, fwd)
    sel = _scan_sel3(q, fwd)
    g = _sel_dot_l(sel, da)
    g_t = g.T
    dtx = _sel_dot_r(dt, ex)
    dax = dtx * (-jnp.exp(parx[0:1]))
    gx = _sel_dot_l(sel, dax)
    glx = jnp.sum(dax, axis=0, keepdims=True)
    xdt = xs * dtx
    xdt_b = xdt.astype(BF16)
    half = (lax.broadcasted_iota(jnp.int32, (1, BW), 1) % (2 * SSM_P)) < SSM_P
    x_even = jnp.where(half, xdt_b, jnp.zeros_like(xdt_b))
    x_odd = jnp.where(half, jnp.zeros_like(xdt_b), xdt_b)
    hpg = SSM_H // SSM_G
    cb = [_bdot_nt(cm[:, k * SSM_N:(k + 1) * SSM_N], bm[:, k * SSM_N:(k + 1) * SSM_N])
          for k in range(SSM_G)]

    def scores(h):
        dec = jnp.exp(jnp.where(incl, g[:, h:h + 1] - g_t[h:h + 1, :], -jnp.inf))
        return (cb[h // hpg] * dec).astype(BF16)

    pairs = []
    for p in range(SSM_H // 2):
        cols = slice(p * 2 * SSM_P, (p + 1) * 2 * SSM_P)
        pairs.append(jnp.dot(scores(2 * p), x_even[:, cols], preferred_element_type=F32)
                     + jnp.dot(scores(2 * p + 1), x_odd[:, cols], preferred_element_type=F32))
    y = jnp.concatenate(pairs, axis=1)
    hb = h_prev.astype(BF16)
    gw = hpg * SSM_P
    y_inter = jnp.concatenate(
        [jnp.dot(cm[:, k * SSM_N:(k + 1) * SSM_N], hb[:, k * gw:(k + 1) * gw], preferred_element_type=F32)
         for k in range(SSM_G)], axis=1)
    y = y + y_inter * jnp.exp(gx)
    xw = (xdt * jnp.exp(glx - gx)).astype(BF16)
    st = jnp.concatenate(
        [_bdot_tn(bm[:, k * SSM_N:(k + 1) * SSM_N], xw[:, k * gw:(k + 1) * gw]) for k in range(SSM_G)],
        axis=1)
    if fwd:
        y = y + parx[1:2] * xs
    return y, jnp.exp(glx) * h_prev + st


def _ssd_kernel(xbc_f_ref, sm_f_ref, xbc_b_ref, sm_b_ref, par_ref, parx_ref, ex_ref, yf_ref, yb_ref, h_ref):
    @pl.when(pl.program_id(1) == 0)
    def _():
        h_ref[...] = jnp.zeros_like(h_ref)

    ex = ex_ref[...]
    y_f, h_f = _ssd_chunk(xbc_f_ref[0], sm_f_ref[0], par_ref[0], parx_ref[0], ex, h_ref[0], True)
    y_b, h_b = _ssd_chunk(xbc_b_ref[0], sm_b_ref[0], par_ref[1], parx_ref[1], ex, h_ref[1], False)
    yf_ref[0] = y_f
    yb_ref[0] = y_b
    h_ref[...] = jnp.stack([h_f, h_b])


def _ssd(conv_out, rest_out, mixpar, ssm_parx, expand, n_lat):
    bsz, s_tot, _ = conv_out.shape
    sm_blk = (REST_W - N_SMALL) // LANES
    chunk_specs, nc = _bidir_specs(n_lat, SSM_Q, [(2 * BW, 0), (LANES, lambda d: sm_blk + d)])
    out_specs, _ = _bidir_specs(n_lat, SSM_Q, [(BW, 0)])
    return pl.pallas_call(
        _ssd_kernel,
        grid=(bsz, nc),
        in_specs=chunk_specs + [_const_spec(mixpar.shape), _const_spec(ssm_parx.shape), _const_spec((LANES, BW))],
        out_specs=out_specs,
        out_shape=[jax.ShapeDtypeStruct((bsz, s_tot, BW), F32)] * 2,
        scratch_shapes=[pltpu.VMEM((2, SSM_N, BW), F32)],
        compiler_params=_cparams(2),
        name="ssd",
    )(conv_out, rest_out, conv_out, rest_out, mixpar, ssm_parx, expand)


def _mlstm_chunk(qk, v, sm, par, c_prev, m_prev_rows, fwd):
    q, nh = ML_Q, ML_H
    hk = nh * ML_DK
    ew = 2 * ML_DV
    pre = sm + par[0:1]
    lf = -_softplus(-pre)
    incl, _ = _scan_masks(q, fwd)
    bt = _sel_dot_l(_scan_sel3(q, fwd), lf)
    pre_t = pre.T
    bt_t = bt.T
    tot = jnp.sum(lf, axis=0, keepdims=True)
    q_cat = qk[:, :hk] * (ML_DK ** -0.5)
    k_cat = qk[:, hk:]
    ones = jnp.ones((q, ML_DV), F32)
    vext = jnp.concatenate([x for h in range(nh) for x in (v[:, h * ML_DV:(h + 1) * ML_DV], ones)], axis=1)
    vext_bd = _block_diag(vext.astype(BF16), nh)
    s_cat = _bdot_nt(q_cat, _block_diag(k_cat.astype(BF16), nh))
    qc = jnp.dot(q_cat.astype(BF16), _block_diag(c_prev.astype(BF16), nh), preferred_element_type=F32)
    w_parts, kw_parts, rows = [], [], []
    for h in range(nh):
        kh = k_cat[:, h * ML_DK:(h + 1) * ML_DK]
        ig_c, ig_r = pre[:, L_IG + h:L_IG + h + 1], pre_t[L_IG + h:L_IG + h + 1, :]
        bt_c, bt_r = bt[:, L_FG + h:L_FG + h + 1], bt_t[L_FG + h:L_FG + h + 1, :]
        bl = tot[:, L_FG + h:L_FG + h + 1]
        dmat = jnp.where(incl, bt_c - bt_r, -jnp.inf) + ig_r
        m_intra = jnp.max(dmat, axis=1, keepdims=True)
        m_loc = jnp.max(bl - bt_r + ig_r, axis=1, keepdims=True)
        kw_parts.append(kh * jnp.exp(bl - bt_c + ig_c - m_loc))
        m_prev = m_prev_rows[h:h + 1, 0:1]
        e = bt_c + m_prev
        m_out = jnp.maximum(e, m_intra)
        w_intra = jnp.exp(dmat - m_out) * s_cat[:, h * q:(h + 1) * q]
        w_parts.append(w_intra)
        m_new = jnp.maximum(bl + m_prev, m_loc)
        rows.append((jnp.exp(e - m_out), jnp.sum(w_intra, axis=1, keepdims=True), m_out, m_new,
                     jnp.exp(bl + m_prev - m_new), jnp.exp(m_loc - m_new)))
    num_cat = jnp.dot(jnp.concatenate(w_parts, axis=1).astype(BF16), _block_diag(v.astype(BF16), nh),
                      preferred_element_type=F32)
    c_loc = _bdot_tn(jnp.concatenate(kw_parts, axis=0), vext_bd)
    outs, m_rows, keep, gain = [], [], [], []
    for h, (w_x, den_intra, m_out, m_new, s_old, s_loc) in enumerate(rows):
        num = num_cat[:, h * ML_DV:(h + 1) * ML_DV] + w_x * qc[:, h * ew:h * ew + ML_DV]
        den = den_intra + w_x * qc[:, h * ew + ML_DV:h * ew + ML_DV + 1]
        outs.append(num / jnp.maximum(jnp.abs(den), jnp.exp(-m_out)))
        m_rows.append(jnp.broadcast_to(m_new, (1, LANES)))
        keep.append(jnp.broadcast_to(s_old, (1, ew)))
        gain.append(jnp.broadcast_to(s_loc, (1, ew)))
    m_rows.append(jnp.zeros((SUBLANES - nh, LANES), F32))
    c_new = jnp.concatenate(keep, axis=1) * c_prev + jnp.concatenate(gain, axis=1) * c_loc
    return jnp.concatenate(outs, axis=1), c_new, jnp.concatenate(m_rows, axis=0)


def _mlstm_kernel(qk_f_ref, v_f_ref, sm_f_ref, qk_b_ref, v_b_ref, sm_b_ref, par_ref, yf_ref, yb_ref, c_ref,
                  m_ref):
    @pl.when(pl.program_id(1) == 0)
    def _():
        c_ref[...] = jnp.zeros_like(c_ref)
        m_ref[...] = jnp.zeros_like(m_ref)

    y_f, c_f, m_f = _mlstm_chunk(qk_f_ref[0], v_f_ref[0], sm_f_ref[0], par_ref[0], c_ref[0], m_ref[0], True)
    y_b, c_b, m_b = _mlstm_chunk(qk_b_ref[0], v_b_ref[0], sm_b_ref[0], par_ref[1], c_ref[1], m_ref[1], False)
    yf_ref[0] = y_f
    yb_ref[0] = y_b
    c_ref[...] = jnp.stack([c_f, c_b])
    m_ref[...] = jnp.stack([m_f, m_b])


def _mlstm(conv_out, rest_out, mixpar, n_lat):
    bsz, s_tot, _ = conv_out.shape
    sm_blk = (REST_W - N_SMALL) // LANES
    chunk_specs, nc = _bidir_specs(n_lat, ML_Q, [(BW, 2), (BW, REST_COL['ml_v']), (LANES, lambda d: sm_blk + d)])
    out_specs, _ = _bidir_specs(n_lat, ML_Q, [(BW, 0)])
    return pl.pallas_call(
        _mlstm_kernel,
        grid=(bsz, nc),
        in_specs=chunk_specs + [_const_spec(mixpar.shape)],
        out_specs=out_specs,
        out_shape=[jax.ShapeDtypeStruct((bsz, s_tot, BW), F32)] * 2,
        scratch_shapes=[pltpu.VMEM((2, ML_DK, ML_H * 2 * ML_DV), F32),
                        pltpu.VMEM((2, SUBLANES, LANES), F32)],
        compiler_params=_cparams(2),
        name="mlstm",
    )(conv_out, rest_out, rest_out, conv_out, rest_out, rest_out, mixpar)


def _block_diag(x, n_blocks):
    r, cols = x.shape
    w = cols // n_blocks
    tiled = jnp.concatenate([x] * n_blocks, axis=0)
    assert r & (r - 1) == 0 and w & (w - 1) == 0
    ri = lax.shift_right_logical(lax.broadcasted_iota(jnp.int32, tiled.shape, 0), r.bit_length() - 1)
    ci = lax.shift_right_logical(lax.broadcasted_iota(jnp.int32, tiled.shape, 1), w.bit_length() - 1)
    return jnp.where(ri == ci, tiled, jnp.zeros_like(tiled))


def _split_hi_lo(x):
    hi = x.astype(BF16)
    return hi, (x - hi.astype(F32)).astype(BF16)


def _dot_hi_blocks(lhs, rhs_cat, n_blocks):
    lh, ll = _split_hi_lo(lhs)
    rh, rl = _split_hi_lo(rhs_cat)
    rh, rl = _block_diag(rh, n_blocks), _block_diag(rl, n_blocks)
    return jnp.dot(jnp.concatenate([lh, lh, ll], axis=1), jnp.concatenate([rh, rl, rh], axis=0),
                   preferred_element_type=F32)


def _unit_lower_inverse_blocks(l_cat, n_blocks):
    q = l_cat.shape[0]
    ri = lax.broadcasted_iota(jnp.int32, l_cat.shape, 0)
    ci = lax.broadcasted_iota(jnp.int32, l_cat.shape, 1) & (q - 1)
    p = -l_cat
    t = jnp.where(ri == ci, 1.0, 0.0) + p
    p = _dot_hi_blocks(p, p, n_blocks)
    levels = max(q.bit_length() - 2, 0)
    for k in range(levels):
        if k == levels - 1:
            return t + _dot_hi_blocks(t, p, n_blocks)
        both = _dot_hi_blocks(jnp.concatenate([p, t], axis=0), p, n_blocks)
        p, t = both[:q], t + both[q:]
    return t


def _head_l2norm(x, width, scale):
    parts = []
    for h in range(x.shape[1] // width):
        seg = x[:, h * width:(h + 1) * width]
        parts.append(seg * (lax.rsqrt(jnp.sum(seg * seg, axis=-1, keepdims=True) + EPS) * scale))
    return jnp.concatenate(parts, axis=1)


def _gdn_chunk(qkvs, sms, pars, s_prev, dirs):
    q, nh = GDN_Q, GDN_H
    nb = nh * len(dirs)
    hk = nh * GDN_DK
    cat = lambda parts: jnp.concatenate(parts, axis=1)
    qn = _head_l2norm(cat([x[:, :hk] for x in qkvs]), GDN_DK, GDN_DK ** -0.5)
    kn = _head_l2norm(cat([x[:, hk:2 * hk] for x in qkvs]), GDN_DK, 1.0)
    v = cat([x[:, 2 * hk:] for x in qkvs])
    col = lambda a, lane0, h, w: jnp.broadcast_to(a[:, lane0 + h:lane0 + h + 1], (q, w))
    ri = lax.broadcasted_iota(jnp.int32, (q, nh * q), 0)
    ci = lax.broadcasted_iota(jnp.int32, (q, nh * q), 1) & (q - 1)
    g_c, gx, bx, g_r, glx, ahead = [], [], [], [], [], []
    for sm, par, fwd in zip(sms, pars, dirs):
        lg = -jnp.exp(par[1:2]) * _softplus(sm + par[0:1])
        beta = _sigmoid(sm)
        g = _sel_dot_l(_scan_sel3(q, fwd), lg)
        g_t = g.T
        gx_d = [col(g, L_GA, h, GDN_DK) for h in range(nh)]
        g_c += [col(g, L_GA, h, q) for h in range(nh)]
        gx += gx_d
        bx += [col(beta, L_GB, h, GDN_DK) for h in range(nh)]
        g_r += [g_t[L_GA + h:L_GA + h + 1, :] for h in range(nh)]
        glx += [x[q - 1:q] if fwd else x[0:1] for x in gx_d]
        ahead.append((ri - ci) if fwd else (ci - ri))
    g_c, gx, bx, g_r, glx, ahead = (cat(x) for x in (g_c, gx, bx, g_r, glx, ahead))
    nh = nb
    hk = nb * GDN_DK
    diff = g_c - g_r
    kb = kn * bx
    k_bd = _block_diag(kn.astype(BF16), nh)
    la = _bdot_nt(jnp.concatenate([kb, qn], axis=0), k_bd)
    l_cat = la[:q] * jnp.exp(jnp.where(ahead > 0, diff, -jnp.inf))
    a_qk = la[q:] * jnp.exp(jnp.where(ahead >= 0, diff, -jnp.inf))
    t_inv = _unit_lower_inverse_blocks(l_cat, nh)
    eg = jnp.exp(gx)
    rhs = jnp.concatenate([_block_diag((v * bx).astype(BF16), nh), _block_diag((kb * eg).astype(BF16), nh)], axis=1)
    uw = jnp.dot(t_inv.astype(BF16), rhs, preferred_element_type=F32)
    ws = jnp.dot(jnp.concatenate([uw[:, hk:], qn * eg], axis=0).astype(BF16),
                 _block_diag(s_prev.astype(BF16), nh), preferred_element_type=F32)
    u_bd = _block_diag((uw[:, :hk] - ws[:q]).astype(BF16), nh)
    kd = kn * jnp.exp(glx - gx)
    kd_t = jnp.concatenate([kd[:, h * GDN_DK:(h + 1) * GDN_DK].T for h in range(nh)], axis=1)
    both = jnp.dot(jnp.concatenate([a_qk, kd_t], axis=0).astype(BF16), u_bd, preferred_element_type=F32)
    return ws[q:] + both[:q], jnp.exp(glx) * s_prev + both[q:]


def _gdn_kernel(qkv_f_ref, sm_f_ref, qkv_b_ref, sm_b_ref, par_ref, yf_ref, yb_ref, s_ref):
    @pl.when(pl.program_id(1) == 0)
    def _():
        s_ref[...] = jnp.zeros_like(s_ref)

    y, s_new = _gdn_chunk((qkv_f_ref[0], qkv_b_ref[0]), (sm_f_ref[0], sm_b_ref[0]), (par_ref[0], par_ref[1]),
                          s_ref[...], (True, False))
    yf_ref[0] = y[:, :BW]
    yb_ref[0] = y[:, BW:]
    s_ref[...] = s_new


def _bidir_specs(n_lat, q, blocks):
    mem_chunk, nc = _chunk_index(n_lat, q)
    specs = []
    for d in range(2):
        for width, col in blocks:
            c = col(d) if callable(col) else col
            specs.append(pl.BlockSpec((1, q, width), lambda b, j, d=d, c=c: (b, mem_chunk(d, j), c)))
    return specs, nc


def _gdn(conv_out, rest_out, mixpar, n_lat):
    bsz, s_tot, _ = conv_out.shape
    sm_blk = (REST_W - N_SMALL) // LANES
    chunk_specs, nc = _bidir_specs(n_lat, GDN_Q, [(3 * BW, 1), (LANES, lambda d: sm_blk + d)])
    out_specs, _ = _bidir_specs(n_lat, GDN_Q, [(BW, 0)])
    return pl.pallas_call(
        _gdn_kernel,
        grid=(bsz, nc),
        in_specs=chunk_specs + [_const_spec(mixpar.shape)],
        out_specs=out_specs,
        out_shape=[jax.ShapeDtypeStruct((bsz, s_tot, BW), F32)] * 2,
        scratch_shapes=[pltpu.VMEM((GDN_DK, 2 * GDN_H * GDN_DV), F32)],
        compiler_params=_cparams(2),
        name="gdn",
    )(conv_out, rest_out, conv_out, rest_out, mixpar)


def _s5_scan(sloc, h_init, pw, fwd, n_valid):
    n = sloc.shape[0]
    row = lax.broadcasted_iota(jnp.int32, (n, 1), 0)

    def mul(x, i):
        return pw[2 * i:2 * i + 1] * x + pw[2 * i + 1:2 * i + 2] * pltpu.roll(x, S5_P, axis=1)

    pos = jnp.where(fwd, row, n - 1 - row)
    first = jnp.where(fwd, 0, n_valid - 1)
    z = pltpu.roll(sloc, jnp.where(fwd, 1, n - 1), axis=0)
    z = jnp.where(pos >= jnp.where(fwd, 1, n - n_valid + 1), z, 0.0)
    z = jnp.where(row == first, h_init, z)
    s, i = 1, 0
    while s < n:
        zs = pltpu.roll(z, jnp.where(fwd, s, n - s), axis=0)
        zs = jnp.where(pos >= s, zs, 0.0)
        z = z + mul(zs, i)
        s, i = 2 * s, i + 1
    last = jnp.where(fwd, n_valid - 1, 0)
    pick = row == last
    z_last = jnp.sum(jnp.where(pick, z, 0.0), axis=0, keepdims=True)
    s_last = jnp.sum(jnp.where(pick, sloc, 0.0), axis=0, keepdims=True)
    return z, mul(z_last, 0) + s_last


def _s5_segment(load_rows, store_rows, n_rows, n_valid, fwd, ut_ref, st_ref, wcat_ref, pm_ref, pw_ref,
                skip_row):
    qg = S5_Q * S5_GC
    for t in range(S5_Q):
        vt = load_rows(t)
        ut_ref[:, t * S5_GC:(t + 1) * S5_GC, :] = vt.T.reshape(S5_G, S5_GC, n_rows)

    def group(gi, carry):
        ug = ut_ref[gi].T[:n_valid].astype(BF16)
        y1 = jnp.dot(ug, wcat_ref[0, gi], preferred_element_type=F32)
        h_start, h_end = _s5_scan(y1[:, qg:], st_ref[gi, 0:1, :], pw_ref[0, gi], fwd, n_valid)
        st_ref[gi, 0:1, :] = h_end
        yy = y1[:, :qg] + jnp.dot(h_start.astype(BF16), pm_ref[0, gi], preferred_element_type=F32)
        if n_valid < n_rows:
            yy = jnp.concatenate([yy, jnp.zeros((n_rows - n_valid, qg), F32)], axis=0)
        ut_ref[gi] = yy.T
        return carry

    lax.fori_loop(0, S5_G, group, 0, unroll=S5_UNROLL)
    for t in range(S5_Q):
        wt = ut_ref[:, t * S5_GC:(t + 1) * S5_GC, :].reshape(BW, n_rows).T
        store_rows(t, wt + skip_row * load_rows(t))


def _load_token(u_ref, t, n_chunks):
    return jnp.concatenate([u_ref[0, k, pl.ds(t, n_chunks, stride=S5_Q), :] for k in range(S5_SPLIT)], axis=1)


def _store_token(y_ref, t, n_chunks, val):
    for k in range(S5_SPLIT):
        y_ref[0, 0, k, pl.ds(t, n_chunks, stride=S5_Q), :] = val[:, k * LANES:(k + 1) * LANES]


def _s5_kernel(ul_ref, uc_ref, wcat_ref, pm_ref, pw_ref, dsk_ref, yl_ref, yc_ref, ut_ref, utc_ref, st_ref,
               *, n_seg_chunks):
    d = pl.program_id(1)
    fwd = d == 0
    skip_row = jnp.where(fwd, dsk_ref[...], 0.0)
    ncc = TM // S5_Q

    @pl.when(pl.program_id(2) == 0)
    def _():
        st_ref[...] = jnp.zeros_like(st_ref)
        pad = jnp.zeros((LANES - ncc, BW), F32)

        def load_c(t):
            return jnp.concatenate([_load_token(uc_ref, t, ncc), pad], axis=0)

        def store_c(t, val):
            _store_token(yc_ref, t, ncc, val[:ncc])

        _s5_segment(load_c, store_c, LANES, ncc, fwd, utc_ref, st_ref, wcat_ref, pm_ref, pw_ref, skip_row)

    def load_l(t):
        return _load_token(ul_ref, t, n_seg_chunks)

    def store_l(t, val):
        _store_token(yl_ref, t, n_seg_chunks, val)

    _s5_segment(load_l, store_l, n_seg_chunks, n_seg_chunks, fwd, ut_ref, st_ref, wcat_ref, pm_ref, pw_ref,
                skip_row)


def _s5(u_slabs, wcat, pm, pw, dskip, n_lat):
    bsz = u_slabs.shape[0]
    seg = min(S5_SEG, n_lat)
    nseg = n_lat // seg
    nsc = seg // S5_Q
    qg = S5_Q * S5_GC
    seg_mem = lambda d, s: jnp.where(d == 0, s, nseg - 1 - s)
    ctx_blk = n_lat // TM
    yl, yc = pl.pallas_call(
        functools.partial(_s5_kernel, n_seg_chunks=nsc),
        grid=(bsz, 2, nseg),
        in_specs=[pl.BlockSpec((1, S5_SPLIT, seg, LANES), lambda b, d, s: (b, 0, seg_mem(d, s), 0)),
                  pl.BlockSpec((1, S5_SPLIT, TM, LANES), lambda b, d, s: (b, 0, ctx_blk, 0)),
                  pl.BlockSpec((1, S5_G, qg, qg + 2 * S5_P), lambda b, d, s: (d, 0, 0, 0)),
                  pl.BlockSpec((1, S5_G, 2 * S5_P, qg), lambda b, d, s: (d, 0, 0, 0)),
                  pl.BlockSpec((1, S5_G, 2 * SUBLANES, LANES), lambda b, d, s: (d, 0, 0, 0)),
                  _const_spec((1, BW))],
        out_specs=[pl.BlockSpec((1, 1, S5_SPLIT, seg, LANES), lambda b, d, s: (d, b, 0, seg_mem(d, s), 0)),
                   pl.BlockSpec((1, 1, S5_SPLIT, TM, LANES), lambda b, d, s: (d, b, 0, 0, 0))],
        out_shape=[jax.ShapeDtypeStruct((2, bsz, S5_SPLIT, n_lat, LANES), F32),
                   jax.ShapeDtypeStruct((2, bsz, S5_SPLIT, TM, LANES), F32)],
        scratch_shapes=[pltpu.VMEM((S5_G, qg, nsc), F32),
                        pltpu.VMEM((S5_G, qg, LANES), F32),
                        pltpu.VMEM((S5_G, SUBLANES, LANES), F32)],
        compiler_params=_cparams(3),
        name="s5",
    )(u_slabs, u_slabs, wcat, pm, pw, dskip)
    return yl, yc


def _s5_operators(a_re, a_im, log_dt, b_re, b_im, c_re, c_im):
    hi = lax.Precision.HIGHEST
    q, nq = S5_Q, S5_Q + 1
    lre = jnp.minimum(a_re.astype(F32), -1e-4)
    lim = a_im.astype(F32)
    dt = jnp.exp(log_dt.astype(F32))[..., None]
    mag = jnp.exp(lre * dt)
    ab_re, ab_im = mag * jnp.cos(lim * dt), mag * jnp.sin(lim * dt)
    den = lre * lre + lim * lim
    f_re = ((ab_re - 1.0) * lre + ab_im * lim) / den
    f_im = (ab_im * lre - (ab_re - 1.0) * lim) / den
    br, bi = b_re.astype(F32), b_im.astype(F32)
    bb_re = f_re[..., None] * br - f_im[..., None] * bi
    bb_im = f_re[..., None] * bi + f_im[..., None] * br
    pr, pi = [jnp.ones_like(ab_re)], [jnp.zeros_like(ab_im)]
    for _ in range(q):
        pr, pi = pr + [pr[-1] * ab_re - pi[-1] * ab_im], pi + [pr[-1] * ab_im + pi[-1] * ab_re]
    pw_re, pw_im = jnp.stack(pr, axis=2), jnp.stack(pi, axis=2)
    e_re = pw_re[..., None] * bb_re[:, :, None] - pw_im[..., None] * bb_im[:, :, None]
    e_im = pw_re[..., None] * bb_im[:, :, None] + pw_im[..., None] * bb_re[:, :, None]
    cr, ci = c_re.astype(F32), c_im.astype(F32)
    kern = (jnp.einsum('gcp,dgnpe->dgnce', cr, e_re, precision=hi)
            - jnp.einsum('gcp,dgnpe->dgnce', ci, e_im, precision=hi))
    s_idx = jnp.arange(q)[:, None]
    t_idx = jnp.arange(q)[None, :]
    mats, rmats, pmats = [], [], []
    for d in range(2):
        lag = (t_idx - s_idx) if d == 0 else (s_idx - t_idx)
        valid = lag >= 0
        m = kern[d][:, jnp.clip(lag, 0, q)]
        m = jnp.where(valid[None, :, :, None, None], m, 0.0)
        mats.append(jnp.transpose(m, (0, 1, 4, 2, 3)).reshape(S5_G, q * S5_GC, q * S5_GC))
        e_s = (q - 1 - jnp.arange(q)) if d == 0 else jnp.arange(q)
        r = jnp.concatenate([e_re[d][:, e_s], e_im[d][:, e_s]], axis=2)
        rmats.append(jnp.transpose(r, (0, 1, 3, 2)).reshape(S5_G, q * S5_GC, 2 * S5_P))
        f_t = (jnp.arange(q) + 1) if d == 0 else (q - jnp.arange(q))
        fr, fi = pw_re[d][:, f_t], pw_im[d][:, f_t]
        p_from_re = cr[:, None] * fr[:, :, None] - ci[:, None] * fi[:, :, None]
        p_from_im = -cr[:, None] * fi[:, :, None] - ci[:, None] * fr[:, :, None]
        p = jnp.concatenate([p_from_re, p_from_im], axis=3)
        pmats.append(jnp.transpose(p, (0, 3, 1, 2)).reshape(S5_G, 2 * S5_P, q * S5_GC))
    wcat = jnp.concatenate([jnp.stack(mats), jnp.stack(rmats)], axis=3).astype(BF16)
    pm = jnp.stack(pmats).astype(BF16)
    dr, di = pw_re[:, :, q], pw_im[:, :, q]
    rows = []
    for _ in range(SUBLANES):
        rows += [jnp.concatenate([dr, dr], axis=-1), jnp.concatenate([-di, di], axis=-1)]
        dr, di = dr * dr - di * di, 2.0 * dr * di
    return wcat, pm, jnp.stack(rows, axis=2)


def _head_rms(x, w, width):
    parts = []
    for h in range(x.shape[1] // width):
        seg = x[:, h * width:(h + 1) * width]
        parts.append(seg * lax.rsqrt(jnp.mean(seg * seg, axis=-1, keepdims=True) + EPS))
    return jnp.concatenate(parts, axis=1) * w


def _merge_kernel(h_ref, mod_ref, nw_ref, wg_ref, gb_ref, yaf_ref, yab_ref, ybf_ref, ybb_ref, yc_ref, ydf_ref, ydb_ref,
                  za_ref, ob_ref,
                  zd_ref, na_ref, nb_ref, nd_ref, glw_ref, glb_ref, wbr_ref, wo_ref, o_ref, *, col_rows):
    x = _tile_rows(h_ref[0], col_rows)
    mod = mod_ref[0]
    xb = (_rms(x, nw_ref[...]) * (1.0 + mod[4:5]) + mod[3:4]).astype(BF16)
    gates = _sigmoid(jnp.dot(xb, wg_ref[...], preferred_element_type=F32) + gb_ref[...])
    ya = _rms((yaf_ref[0] + yab_ref[0]) * _silu(za_ref[0]), na_ref[...])
    yb = _sigmoid(ob_ref[0]) * _head_rms(ybf_ref[0] + ybb_ref[0], nb_ref[...], ML_DV)
    gc = _gelu_tanh(jnp.concatenate([yc_ref[0, 0, k] + yc_ref[1, 0, k] for k in range(S5_SPLIT)], axis=1))
    yc = gc * _sigmoid(_bdot(gc, glw_ref[...]) + glb_ref[...])
    yd = _head_rms(ydf_ref[0] + ydb_ref[0], nd_ref[...], GDN_DV) * _silu(zd_ref[0])
    merged = jnp.zeros((TM, D), F32)
    for k, yk in enumerate((ya, yb, yc, yd)):
        merged = merged + gates[:, k * D:(k + 1) * D] * _bdot(yk, wbr_ref[k])
    out = _bdot(merged, wo_ref[...])
    o_ref[0] = _tile_unrows(x + mod[5:6] * out, col_rows)


def _merge(h, mod, mod_row, tile0, nw, wg, gb, ys, rest_out, norms, glw, glb, wbr, wo, col_rows):
    bsz, n, _ = h.shape
    nt = n // TM
    tile_shape, view = _lat_tile_spec(n, col_rows)
    tile_map = (lambda b, t: (b, t, 0)) if col_rows is None else (lambda b, t: (b, 0, t))
    row = (lambda b: b) if mod_row is None else (lambda b: mod_row)
    s5_spec = lambda t0: pl.BlockSpec((2, 1, S5_SPLIT, TM, LANES), lambda b, t: (0, b, 0, t0 + t, 0))
    one_spec = lambda t0: pl.BlockSpec((1, TM, BW), lambda b, t: (b, t0 + t, 0))
    specs = (one_spec, one_spec, one_spec, one_spec, s5_spec, one_spec, one_spec)
    r_spec = lambda c: pl.BlockSpec((1, TM, BW), lambda b, t: (b, tile0 + t, c))
    hv = view(h)
    out = pl.pallas_call(
        functools.partial(_merge_kernel, col_rows=col_rows),
        grid=(bsz, nt),
        in_specs=[pl.BlockSpec(tile_shape, tile_map),
                  pl.BlockSpec((1, N_MOD, D), lambda b, t: (row(b), 0, 0)),
                  _const_spec((1, D)),
                  _const_spec((D, N_BRANCH * D)),
                  _const_spec((1, N_BRANCH * D)),
                  *[mk(t0) for mk, (_, t0) in zip(specs, ys)],
                  r_spec(REST_COL['ssm_z']), r_spec(REST_COL['ml_o']), r_spec(REST_COL['gdn_z']),
                  _const_spec((1, BW)), _const_spec((1, BW)), _const_spec((1, BW)),
                  _const_spec((BW, BW)), _const_spec((1, BW)),
                  _const_spec((N_BRANCH, BW, D)),
                  _const_spec((D, D))],
        out_specs=pl.BlockSpec(tile_shape, tile_map),
        out_shape=jax.ShapeDtypeStruct(hv.shape, F32),
        compiler_params=_cparams(2),
        name="merge",
    )(hv, mod, nw.reshape(1, D), wg, gb, *[a for a, _ in ys], rest_out, rest_out, rest_out, *norms, glw, glb, wbr, wo)
    return out.reshape(h.shape)


def _layer_params(l, p):
    offs = [0]
    for w in IN_SPLITS:
        offs.append(offs[-1] + w)
    w_in = p['w_in'][l]
    col = lambda i: w_in[:, offs[i]:offs[i + 1]]
    z128 = lambda k: jnp.zeros((D, k), F32)

    def small(d):
        dt = col(2).reshape(D, 2, SSM_H)[:, d]
        gates = col(6).reshape(D, 2, 2 * ML_H)[:, d]
        ga = col(10).reshape(D, 2, GDN_H)[:, d]
        gb = col(11).reshape(D, 2, GDN_H)[:, d]
        used = SSM_H + 2 * ML_H + 2 * GDN_H
        return jnp.concatenate([dt, gates, ga, gb, z128(LANES - used)], axis=1)

    wc = jnp.concatenate([col(1), col(3), col(8)], axis=1).astype(BF16)
    wr = jnp.concatenate([col(0), col(4), col(5), col(9), small(0), small(1), col(7)], axis=1).astype(BF16)
    cw = jnp.concatenate([p['ssm_conv_w'][l], p['ml_conv_w'][l], p['gdn_conv_w'][l]], axis=1)
    cw = jnp.concatenate([cw, jnp.zeros((SUBLANES - CONV_K, CONV_W), F32)], axis=0)
    cb = jnp.concatenate([p['ssm_conv_b'][l], p['ml_conv_b'][l], jnp.zeros((GDN_H * (2 * GDN_DK + GDN_DV),), F32)])

    def par_rows(d):
        pad = jnp.zeros((LANES - L_GB,), F32)
        bias = jnp.concatenate([p['ssm_dt_bias'][l, d], p['ml_gate_b'][l, d].reshape(-1),
                                p['gdn_dt_bias'][l, d], pad])
        alog = jnp.concatenate([p['ssm_a_log'][l, d], jnp.zeros((2 * ML_H,), F32), p['gdn_a_log'][l, d], pad])
        return jnp.concatenate([bias[None], alog[None], jnp.zeros((SUBLANES - 2, LANES), F32)], axis=0)

    mixpar = jnp.stack([par_rows(0), par_rows(1)])

    def parx_rows(d):
        rows = jnp.stack([jnp.repeat(p['ssm_a_log'][l, d], SSM_P), jnp.repeat(p['ssm_d'][l], SSM_P)])
        return jnp.concatenate([rows, jnp.zeros((SUBLANES - 2, BW), F32)], axis=0)

    ssm_parx = jnp.stack([parx_rows(0), parx_rows(1)])
    wcat, pm, pw = _s5_operators(p['s5_a_re'][l], p['s5_a_im'][l], p['s5_log_dt'][l], p['s5_b_re'][l],
                                 p['s5_b_im'][l], p['s5_c_re'][l], p['s5_c_im'][l])
    return dict(
        wc=wc, wr=wr, cw=cw, cb=cb.reshape(1, CONV_W), mixpar=mixpar, ssm_parx=ssm_parx,
        s5_wcat=wcat, s5_pm=pm, s5_pw=pw, s5_d=p['s5_d'][l].reshape(1, BW),
        wg=col(12).astype(BF16), gb=p['gate_b'][l].reshape(1, N_BRANCH * D),
        norms=(p['ssm_norm'][l].reshape(1, BW), p['ml_norm'][l].reshape(1, BW),
               jnp.tile(p['gdn_norm'][l], GDN_H).reshape(1, BW)),
        glw=p['s5_glu_w'][l].astype(BF16), glb=p['s5_glu_b'][l].reshape(1, BW),
        wbr=p['w_branch'][l].astype(BF16), wo=p['w_out'][l].astype(BF16),
        ffn_up=p['ffn_up'][l].astype(BF16), ffn_dn=p['ffn_down'][l].astype(BF16))


def kernel(x, c, ctx, c_ctx, ada_w, ada_b, norm_w, ffn_up, ffn_down, w_in, ssm_conv_w, ssm_conv_b, ssm_dt_bias, ssm_a_log, ssm_d, ssm_norm, ml_conv_w, ml_conv_b, ml_gate_b, ml_norm, s5_a_re, s5_a_im, s5_log_dt, s5_b_re, s5_b_im, s5_c_re, s5_c_im, s5_d, s5_glu_w, s5_glu_b, gdn_conv_w, gdn_dt_bias, gdn_a_log, gdn_norm, gate_b, w_branch, w_out, final_norm):
    p = dict(ffn_up=ffn_up, ffn_down=ffn_down, w_in=w_in, ssm_conv_w=ssm_conv_w, ssm_conv_b=ssm_conv_b,
             ssm_dt_bias=ssm_dt_bias, ssm_a_log=ssm_a_log, ssm_d=ssm_d, ssm_norm=ssm_norm, ml_conv_w=ml_conv_w,
             ml_conv_b=ml_conv_b, ml_gate_b=ml_gate_b, ml_norm=ml_norm, s5_a_re=s5_a_re, s5_a_im=s5_a_im,
             s5_log_dt=s5_log_dt, s5_b_re=s5_b_re, s5_b_im=s5_b_im, s5_c_re=s5_c_re, s5_c_im=s5_c_im, s5_d=s5_d,
             s5_glu_w=s5_glu_w, s5_glu_b=s5_glu_b, gdn_conv_w=gdn_conv_w, gdn_dt_bias=gdn_dt_bias,
             gdn_a_log=gdn_a_log, gdn_norm=gdn_norm, gate_b=gate_b, w_branch=w_branch, w_out=w_out)
    bsz, n, _ = x.shape
    depth = ada_w.shape[0]
    rows = n // GRID_W
    assert ctx.shape[1] == TM and n % TM == 0 and bsz + 1 <= SUBLANES
    assert TM % rows == 0 and rows % HALO == 0 and n % min(S5_SEG, n) == 0
    cvec = jnp.concatenate([c, c_ctx[None], jnp.zeros((SUBLANES - bsz - 1, D), F32)], axis=0)
    modtab = _mod_table(cvec, ada_w, ada_b)
    lane_head = jnp.arange(LANES)[:, None] == (jnp.arange(BW)[None, :] // SSM_P)
    expand = lane_head.astype(BF16)
    h_lat, h_ctx = x, ctx
    nt = n // TM
    for l in range(depth):
        lp = _layer_params(l, p)
        mod = modtab[l]
        need_ctx = l < depth - 1
        col_rows = rows if l % 2 == 1 else None
        h_lat = _ffn(h_lat, mod, None, norm_w[l, 0], lp['ffn_up'][0], lp['ffn_dn'][0], 0)
        h_ctx = _ffn(h_ctx, mod, bsz, norm_w[l, 0], lp['ffn_up'][0], lp['ffn_dn'][0], 0)
        conv_out, rest_out, u_s5 = _inproj(h_lat, h_ctx, mod, norm_w[l, 1], lp['wc'], lp['wr'], lp['cw'],
                                           lp['cb'], col_rows)
        y_ssd = _ssd(conv_out, rest_out, lp['mixpar'], lp['ssm_parx'], expand, n)
        y_ml = _mlstm(conv_out, rest_out, lp['mixpar'], n)
        y_s5_lat, y_s5_ctx = _s5(u_s5, lp['s5_wcat'], lp['s5_pm'], lp['s5_pw'], lp['s5_d'], n)
        y_gdn = _gdn(conv_out, rest_out, lp['mixpar'], n)
        tail = (rest_out, lp['norms'], lp['glw'], lp['glb'], lp['wbr'], lp['wo'])
        head = (norm_w[l, 1], lp['wg'], lp['gb'])
        ys_lat = ((y_ssd[0], 0), (y_ssd[1], 0), (y_ml[0], 0), (y_ml[1], 0), (y_s5_lat, 0), (y_gdn[0], 0), (y_gdn[1], 0))
        ys_ctx = ((y_ssd[0], nt), (y_ssd[1], nt), (y_ml[0], nt), (y_ml[1], nt), (y_s5_ctx, 0), (y_gdn[0], nt), (y_gdn[1], nt))
        h_lat = _merge(h_lat, mod, None, 0, *head, ys_lat, *tail, col_rows)
        last = l == depth - 1
        h_lat = _ffn(h_lat, mod, None, norm_w[l, 2], lp['ffn_up'][1], lp['ffn_dn'][1], 6,
                     final_w=final_norm if last else None)
        if need_ctx:
            h_ctx = _merge(h_ctx, mod, bsz, nt, *head, ys_ctx, *tail, None)
            h_ctx = _ffn(h_ctx, mod, bsz, norm_w[l, 2], lp['ffn_up'][1], lp['ffn_dn'][1], 6)
    return h_lat
```
